```python
import jax
import jax.numpy as jnp
from jax import lax
import numpy as np

D_MODEL = 1024
BATCH = 4
SEQ = 4096
DEPTH = 4
DEC_BATCH = 128
DEC_SEQ = 4
PAST_LEN = 2048
PAGE_SIZE = 128

N_MIXERS = 2
N_CONV_LAYERS = (DEPTH + N_MIXERS - 1) // N_MIXERS
N_NSA_LAYERS = DEPTH // N_MIXERS
CONV_WIDTH = 31
N_HEADS = 16
HEAD_DIM = D_MODEL // N_HEADS
KV_GROUPS = 4
GROUP_SIZE = N_HEADS // KV_GROUPS
N_BRANCH = 3
CMP_BLOCK = 32
CMP_STRIDE = 16
CMP_HIDDEN = 2 * HEAD_DIM
SEL_BLOCK = 64
N_SELECT = 16
WINDOW = 512
Q_BLOCK = 64
D_FF = 4 * D_MODEL
PLE_DIM = 256
ROPE_THETA = 10000.0
EPS = 1e-6
Q_COLS = N_HEADS * HEAD_DIM
KV_COLS = 2 * N_BRANCH * KV_GROUPS * HEAD_DIM
GATE_COLS = N_BRANCH * N_HEADS
NSA_IN_COLS = Q_COLS + KV_COLS + GATE_COLS

kernel_name = 'hybrid_conformer_nsa_decode_step'


def rmsnorm(x, g):
    xf = x.astype(jnp.float32)
    y = xf * lax.rsqrt(jnp.mean(xf * xf, axis=-1, keepdims=True) + EPS)
    return (y * g.astype(jnp.float32)).astype(x.dtype)


def rope(x, pos):
    half = x.shape[-1] // 2
    inv = ROPE_THETA ** (-jnp.arange(half, dtype=jnp.float32) / half)
    ang = pos[:, None] * inv[None, :]
    c = jnp.cos(ang)[:, None, :]
    s = jnp.sin(ang)[:, None, :]
    xf = x.astype(jnp.float32)
    x1, x2 = xf[..., :half], xf[..., half:]
    return jnp.concatenate([x1 * c - x2 * s, x2 * c + x1 * s], axis=-1).astype(x.dtype)


def masked_softmax(s, mask):
    s = jnp.where(mask, s.astype(jnp.float32), -jnp.inf)
    m = jnp.max(s, axis=-1, keepdims=True)
    m = jnp.where(jnp.isfinite(m), m, 0.0)
    e = jnp.exp(s - m)
    d = jnp.sum(e, axis=-1, keepdims=True)
    return e / jnp.where(d > 0, d, 1.0)


def conv_module(u, state, w_in, b_in, w_dw, b_dw, ln_g, ln_b, w_out, b_out):
    a = u @ w_in + b_in
    glu = a[..., :D_MODEL] * jax.nn.sigmoid(a[..., D_MODEL:])
    xin = jnp.concatenate([state.astype(glu.dtype), glu], axis=1)
    y = lax.conv_general_dilated(xin, w_dw[:, None, :], window_strides=(1,), padding='VALID',
                                 dimension_numbers=('NWC', 'WIO', 'NWC'),
                                 feature_group_count=D_MODEL) + b_dw
    yf = y.astype(jnp.float32)
    mu = jnp.mean(yf, axis=-1, keepdims=True)
    var = jnp.mean(jnp.square(yf - mu), axis=-1, keepdims=True)
    yn = (yf - mu) * lax.rsqrt(var + EPS) * ln_g.astype(jnp.float32) + ln_b.astype(jnp.float32)
    z = jax.nn.silu(yn).astype(u.dtype) @ w_out + b_out
    return z, xin[:, -(CONV_WIDTH - 1):]


def compress(x, pe, w1, w2):
    b, l, g, dh = x.shape
    n_chunk = -(-l // CMP_STRIDE)
    xp = jnp.pad(x, ((0, 0), (0, n_chunk * CMP_STRIDE - l), (0, 0), (0, 0)))
    ch = xp.reshape(b, n_chunk, CMP_STRIDE, g, dh)
    r = CMP_BLOCK // CMP_STRIDE
    n_cmp = n_chunk - r + 1
    blocks = jnp.concatenate([ch[:, o:o + n_cmp] for o in range(r)], axis=2)
    blocks = blocks + pe[:, None, :]
    flat = blocks.transpose(0, 1, 3, 2, 4).reshape(b, n_cmp, g, CMP_BLOCK * dh)
    return jax.nn.gelu(flat @ w1) @ w2


def cover_matrix(n_cmp, n_sel):
    cs = np.arange(n_cmp) * CMP_STRIDE
    ss = np.arange(n_sel) * SEL_BLOCK
    m = (cs[:, None] < ss[None, :] + SEL_BLOCK) & (cs[:, None] + CMP_BLOCK > ss[None, :])
    return jnp.asarray(m, dtype=jnp.float32)


def nsa_mixer(u, q_offset, pk_c, pv_c, pk_s, pv_s, kw_buf, vw_buf,
              w_in, b_gate, pe_k, pe_v, wk1, wk2, wv1, wv2, w_o):
    b, s, _ = u.shape
    proj = u @ w_in
    q = proj[..., :Q_COLS].reshape(b, s, N_HEADS, HEAD_DIM)
    kv = proj[..., Q_COLS:Q_COLS + KV_COLS].reshape(b, s, 2 * N_BRANCH, KV_GROUPS, HEAD_DIM)
    gates = jax.nn.sigmoid(proj[..., Q_COLS + KV_COLS:] + b_gate).reshape(b, s, KV_GROUPS, GROUP_SIZE, N_BRANCH)
    qpos = q_offset + jnp.arange(s, dtype=jnp.int32)
    fpos = qpos.astype(jnp.float32)
    q_c = q.reshape(b, s, KV_GROUPS, GROUP_SIZE, HEAD_DIM)
    q_r = rope(q, fpos).reshape(b, s, KV_GROUPS, GROUP_SIZE, HEAD_DIM)
    k_c, v_c = kv[:, :, 0], kv[:, :, 1]
    k_s, v_s = rope(kv[:, :, 2], fpos), kv[:, :, 3]
    k_w, v_w = rope(kv[:, :, 4], fpos), kv[:, :, 5]

    ck = compress(jnp.concatenate([pk_c, k_c], axis=1), pe_k, wk1, wk2)
    cv = compress(jnp.concatenate([pv_c, v_c], axis=1), pe_v, wv1, wv2)
    n_cmp = ck.shape[1]
    c_end = jnp.arange(n_cmp, dtype=jnp.int32) * CMP_STRIDE + (CMP_BLOCK - 1)

    ks_all = jnp.concatenate([pk_s, k_s], axis=1)
    vs_all = jnp.concatenate([pv_s, v_s], axis=1)
    l_k = ks_all.shape[1]
    n_sel = -(-l_k // SEL_BLOCK)
    pad = n_sel * SEL_BLOCK - l_k

    def to_sel_blocks(a):
        a = jnp.pad(a, ((0, 0), (0, pad), (0, 0), (0, 0)))
        return a.reshape(b, n_sel, SEL_BLOCK, KV_GROUPS, HEAD_DIM).transpose(0, 3, 1, 2, 4)

    ks_blk = to_sel_blocks(ks_all)
    vs_blk = to_sel_blocks(vs_all)
    cover = cover_matrix(n_cmp, n_sel)
    n_top = min(N_SELECT, n_sel)
    bidx = jnp.arange(b)[:, None, None, None]
    gidx = jnp.arange(KV_GROUPS)[None, :, None, None]
    blk = jnp.arange(n_sel, dtype=jnp.int32)

    wb = kw_buf.shape[1]
    wpad = jnp.zeros((b, WINDOW - wb, KV_GROUPS, HEAD_DIM), k_w.dtype)
    kwa = jnp.concatenate([wpad, kw_buf, k_w], axis=1)
    vwa = jnp.concatenate([wpad, vw_buf, v_w], axis=1)

    qb = Q_BLOCK if s % Q_BLOCK == 0 else s
    n_blk = s // qb
    scale = HEAD_DIM ** -0.5

    def body(xs):
        q_b, qr_b, g_b, n = xs
        start = n * qb
        qp = q_offset + start + jnp.arange(qb, dtype=jnp.int32)
        s_c = jnp.einsum('bqgrd,bngd->bqgrn', q_b, ck) * scale
        p_c = masked_softmax(s_c, (c_end[None, :] <= qp[:, None])[None, :, None, None, :])
        o_c = jnp.einsum('bqgrn,bngd->bqgrd', p_c, cv)
        imp = jnp.einsum('bqgrn,ns->bqgs', p_c, cover)
        cur = qp // SEL_BLOCK
        forced = (blk[None, :] == 0) | (blk[None, :] == cur[:, None]) | (blk[None, :] == cur[:, None] - 1)
        future = blk[None, :] > cur[:, None]
        score = jnp.where(future[None, :, None, :], -jnp.inf,
                          jnp.where(forced[None, :, None, :], jnp.inf, imp))
        _, top = lax.top_k(score, n_top)
        top = top.transpose(0, 2, 1, 3)
        kb = ks_blk[bidx, gidx, top]
        vb = vs_blk[bidx, gidx, top]
        kpos = top[..., None] * SEL_BLOCK + jnp.arange(SEL_BLOCK, dtype=jnp.int32)
        m_s = (kpos <= qp[None, None, :, None, None]).transpose(0, 2, 1, 3, 4)
        m_s = m_s.reshape(b, qb, KV_GROUPS, 1, n_top * SEL_BLOCK)
        s_s = jnp.einsum('bqgrd,bgqksd->bqgrks', qr_b, kb).reshape(b, qb, KV_GROUPS, GROUP_SIZE, n_top * SEL_BLOCK) * scale
        p_s = masked_softmax(s_s, m_s).reshape(b, qb, KV_GROUPS, GROUP_SIZE, n_top, SEL_BLOCK)
        o_s = jnp.einsum('bqgrks,bgqksd->bqgrd', p_s, vb)
        kw_b = lax.dynamic_slice_in_dim(kwa, start, WINDOW + qb, axis=1)
        vw_b = lax.dynamic_slice_in_dim(vwa, start, WINDOW + qb, axis=1)
        wpos = q_offset - WINDOW + start + jnp.arange(WINDOW + qb, dtype=jnp.int32)
        m_w = (wpos[None, :] >= 0) & (wpos[None, :] <= qp[:, None]) & (wpos[None, :] > qp[:, None] - WINDOW)
        s_w = jnp.einsum('bqgrd,bkgd->bqgrk', qr_b, kw_b) * scale
        p_w = masked_softmax(s_w, m_w[None, :, None, None, :])
        o_w = jnp.einsum('bqgrk,bkgd->bqgrd', p_w, vw_b)
        return g_b[..., 0:1] * o_c + g_b[..., 1:2] * o_s + g_b[..., 2:3] * o_w

    def to_blocks(a):
        return jnp.moveaxis(a.reshape((b, n_blk, qb) + a.shape[2:]), 1, 0)

    o = lax.map(body, (to_blocks(q_c), to_blocks(q_r), to_blocks(gates), jnp.arange(n_blk, dtype=jnp.int32)))
    o = jnp.moveaxis(o, 0, 1).reshape(b, s, Q_COLS).astype(u.dtype)
    keep = min(WINDOW, l_k)
    return o @ w_o, (k_c, v_c, k_s, v_s, kwa[:, -keep:], vwa[:, -keep:])


def gather_pages(pool, page_table):
    g = pool[page_table]
    return g.reshape(page_table.shape[0], page_table.shape[1] * pool.shape[1], pool.shape[2], pool.shape[3])


def channel_and_ple(h, p_i, g_pre, g_post, w_ff_in, w_ff_out, w_pg, w_pp):
    u = rmsnorm(h, g_pre)
    f = jnp.square(jax.nn.relu(u @ w_ff_in)) @ w_ff_out
    h = h + rmsnorm(f, g_post)
    return h + jax.nn.sigmoid(h @ w_pg) * (p_i @ w_pp)


def setup_inputs(seed: int = 0) -> dict:
    key = jax.random.key(seed)
    keys = iter(jax.random.split(key, 64))

    def nrm(shape, scale=1.0):
        return jax.random.normal(next(keys), shape, jnp.float32) * scale

    def gain(shape):
        return 1.0 + nrm(shape, 0.02)

    n_pages = PAST_LEN // PAGE_SIZE
    n_used = DEC_BATCH * n_pages
    n_pool = n_used + max(1, n_used // 4)
    wb = min(WINDOW, PAST_LEN)
    pool = (N_NSA_LAYERS, n_pool, PAGE_SIZE, KV_GROUPS, HEAD_DIM)
    win = (N_NSA_LAYERS, DEC_BATCH, wb, KV_GROUPS, HEAD_DIM)
    page_table = jax.random.permutation(next(keys), n_pool)[:n_used].reshape(DEC_BATCH, n_pages).astype(jnp.int32)
    return {
        'x_prompt': nrm((BATCH, SEQ, D_MODEL)),
        'x_sample': nrm((DEC_BATCH, DEC_SEQ, D_MODEL)),
        'state_conv': nrm((N_CONV_LAYERS, DEC_BATCH, CONV_WIDTH - 1, D_MODEL), 0.5),
        'cache_k_cmp': nrm(pool),
        'cache_v_cmp': nrm(pool),
        'cache_k_sel': nrm(pool),
        'cache_v_sel': nrm(pool),
        'state_k_win': nrm(win),
        'state_v_win': nrm(win),
        'page_table': page_table,
        'p_prompt': nrm((DEPTH, BATCH, SEQ, PLE_DIM)),
        'p_sample': nrm((DEPTH, DEC_BATCH, DEC_SEQ, PLE_DIM)),
        'norm_mix_pre': gain((DEPTH, D_MODEL)),
        'norm_mix_post': gain((DEPTH, D_MODEL)),
        'norm_ffn_pre': gain((DEPTH, D_MODEL)),
        'norm_ffn_post': gain((DEPTH, D_MODEL)),
        'conv_w_in': nrm((N_CONV_LAYERS, D_MODEL, 2 * D_MODEL), D_MODEL ** -0.5),
        'conv_b_in': nrm((N_CONV_LAYERS, 2 * D_MODEL), 0.02),
        'conv_w_dw': nrm((N_CONV_LAYERS, CONV_WIDTH, D_MODEL), CONV_WIDTH ** -0.5),
        'conv_b_dw': nrm((N_CONV_LAYERS, D_MODEL), 0.02),
        'conv_ln_g': gain((N_CONV_LAYERS, D_MODEL)),
        'conv_ln_b': nrm((N_CONV_LAYERS, D_MODEL), 0.02),
        'conv_w_out': nrm((N_CONV_LAYERS, D_MODEL, D_MODEL), D_MODEL ** -0.5),
        'conv_b_out': nrm((N_CONV_LAYERS, D_MODEL), 0.02),
        'nsa_w_in': nrm((N_NSA_LAYERS, D_MODEL, NSA_IN_COLS), D_MODEL ** -0.5),
        'nsa_b_gate': nrm((N_NSA_LAYERS, GATE_COLS), 0.02),
        'nsa_pe_k': nrm((N_NSA_LAYERS, CMP_BLOCK, HEAD_DIM), 0.5),
        'nsa_pe_v': nrm((N_NSA_LAYERS, CMP_BLOCK, HEAD_DIM), 0.5),
        'nsa_wk_c1': nrm((N_NSA_LAYERS, CMP_BLOCK * HEAD_DIM, CMP_HIDDEN), (CMP_BLOCK * HEAD_DIM) ** -0.5),
        'nsa_wk_c2': nrm((N_NSA_LAYERS, CMP_HIDDEN, HEAD_DIM), CMP_HIDDEN ** -0.5),
        'nsa_wv_c1': nrm((N_NSA_LAYERS, CMP_BLOCK * HEAD_DIM, CMP_HIDDEN), (CMP_BLOCK * HEAD_DIM) ** -0.5),
        'nsa_wv_c2': nrm((N_NSA_LAYERS, CMP_HIDDEN, HEAD_DIM), CMP_HIDDEN ** -0.5),
        'nsa_w_o': nrm((N_NSA_LAYERS, Q_COLS, D_MODEL), Q_COLS ** -0.5),
        'ffn_w_in': nrm((DEPTH, D_MODEL, D_FF), D_MODEL ** -0.5),
        'ffn_w_out': nrm((DEPTH, D_FF, D_MODEL), D_FF ** -0.5),
        'ple_w_gate': nrm((DEPTH, D_MODEL, D_MODEL), D_MODEL ** -0.5),
        'ple_w_proj': nrm((DEPTH, PLE_DIM, D_MODEL), PLE_DIM ** -0.5),
    }


def reference(x_prompt, x_sample, state_conv, cache_k_cmp, cache_v_cmp, cache_k_sel, cache_v_sel,
              state_k_win, state_v_win, page_table, p_prompt, p_sample,
              norm_mix_pre, norm_mix_post, norm_ffn_pre, norm_ffn_post,
              conv_w_in, conv_b_in, conv_w_dw, conv_b_dw, conv_ln_g, conv_ln_b, conv_w_out, conv_b_out,
              nsa_w_in, nsa_b_gate, nsa_pe_k, nsa_pe_v, nsa_wk_c1, nsa_wk_c2, nsa_wv_c1, nsa_wv_c2, nsa_w_o,
              ffn_w_in, ffn_w_out, ple_w_gate, ple_w_proj):
    past_len = page_table.shape[1] * cache_k_cmp.shape[2]
    n_p = x_prompt.shape[0]
    hp, hs = x_prompt, x_sample
    conv_p, conv_s, nsa_p, nsa_s = [], [], [], []
    for i in range(DEPTH):
        j = i // N_MIXERS
        up = rmsnorm(hp, norm_mix_pre[i])
        us = rmsnorm(hs, norm_mix_pre[i])
        if i % N_MIXERS == 0:
            cw = (conv_w_in[j], conv_b_in[j], conv_w_dw[j], conv_b_dw[j],
                  conv_ln_g[j], conv_ln_b[j], conv_w_out[j], conv_b_out[j])
            mp, st_p = conv_module(up, jnp.zeros((n_p, CONV_WIDTH - 1, D_MODEL), up.dtype), *cw)
            ms, st_s = conv_module(us, state_conv[j], *cw)
            conv_p.append(st_p)
            conv_s.append(st_s)
        else:
            nw = (nsa_w_in[j], nsa_b_gate[j], nsa_pe_k[j], nsa_pe_v[j],
                  nsa_wk_c1[j], nsa_wk_c2[j], nsa_wv_c1[j], nsa_wv_c2[j], nsa_w_o[j])
            empty = jnp.zeros((n_p, 0, KV_GROUPS, HEAD_DIM), up.dtype)
            mp, st_p = nsa_mixer(up, 0, empty, empty, empty, empty, empty, empty, *nw)
            pkc = gather_pages(cache_k_cmp[j], page_table)
            pvc = gather_pages(cache_v_cmp[j], page_table)
            pks = gather_pages(cache_k_sel[j], page_table)
            pvs = gather_pages(cache_v_sel[j], page_table)
            ms, st_s = nsa_mixer(us, past_len, pkc, pvc, pks, pvs, state_k_win[j], state_v_win[j], *nw)
            nsa_p.append(st_p)
            nsa_s.append(st_s)
        fw = (norm_ffn_pre[i], norm_ffn_post[i], ffn_w_in[i], ffn_w_out[i], ple_w_gate[i], ple_w_proj[i])
        hp = channel_and_ple(hp + rmsnorm(mp, norm_mix_post[i]), p_prompt[i], *fw)
        hs = channel_and_ple(hs + rmsnorm(ms, norm_mix_post[i]), p_sample[i], *fw)

    conv_state_p = jnp.stack(conv_p)
    conv_state_s = jnp.stack(conv_s)
    k_cmp_p = jnp.stack([st[0] for st in nsa_p])
    v_cmp_p = jnp.stack([st[1] for st in nsa_p])
    k_sel_p = jnp.stack([st[2] for st in nsa_p])
    v_sel_p = jnp.stack([st[3] for st in nsa_p])
    k_win_p = jnp.stack([st[4] for st in nsa_p])
    v_win_p = jnp.stack([st[5] for st in nsa_p])
    k_cmp_s = jnp.stack([st[0] for st in nsa_s])
    v_cmp_s = jnp.stack([st[1] for st in nsa_s])
    k_sel_s = jnp.stack([st[2] for st in nsa_s])
    v_sel_s = jnp.stack([st[3] for st in nsa_s])
    k_win_s = jnp.stack([st[4] for st in nsa_s])
    v_win_s = jnp.stack([st[5] for st in nsa_s])
    return (hp, hs, conv_state_p, k_cmp_p, v_cmp_p, k_sel_p, v_sel_p, k_win_p, v_win_p,
            conv_state_s, k_cmp_s, v_cmp_s, k_sel_s, v_sel_s, k_win_s, v_win_s)
```

```python
import functools

import numpy as np
import jax
import jax.numpy as jnp
from jax import lax
from jax.experimental import pallas as pl
from jax.experimental.pallas import tpu as pltpu

F32 = jnp.float32
BF16 = jnp.bfloat16

D_MODEL = 1024
DEPTH = 4
N_MIXERS = 2
CONV_WIDTH = 31
N_HEADS = 16
HEAD_DIM = 64
KV_GROUPS = 4
GROUP_SIZE = 4
N_BRANCH = 3
CMP_BLOCK = 32
CMP_STRIDE = 16
CMP_HIDDEN = 2 * HEAD_DIM
SEL_BLOCK = 64
N_SELECT = 16
WINDOW = 512
D_FF = 4 * D_MODEL
ROPE_THETA = 10000.0
EPS = 1e-6
KV_W = KV_GROUPS * HEAD_DIM
Q_COLS = N_HEADS * HEAD_DIM
KV_COLS = 2 * N_BRANCH * KV_W
GATE_COLS = N_BRANCH * N_HEADS
GATE_PAD = 128
NEG = -1e30
MAX_SEL_BLOCKS = 64
VMEM_LIMIT = 56 * 1024 * 1024


def _cparams(*sem):
    return pltpu.CompilerParams(dimension_semantics=sem, vmem_limit_bytes=VMEM_LIMIT)


def _rms(x, g):
    return x * lax.rsqrt(jnp.mean(x * x, axis=-1, keepdims=True) + EPS) * g


def _dot(a, b):
    return jnp.dot(a, b, preferred_element_type=F32)


def _dot_t(a, b):
    return lax.dot_general(a, b, (((1,), (1,)), ((), ())), preferred_element_type=F32)


def _split_bf16(x):
    hi = x.astype(BF16)
    lo = (x - hi.astype(F32)).astype(BF16)
    return hi, lo


COL_TILE = 256


def _conv_in_body(x_ref, g_ref, w_ref, b_ref, o_ref):
    u = _rms(x_ref[...], g_ref[...]).astype(BF16)
    for j in range(D_MODEL // COL_TILE):
        lo, hi = j * COL_TILE, (j + 1) * COL_TILE
        a1 = _dot(u, w_ref[:, lo:hi]) + b_ref[:, lo:hi]
        a2 = _dot(u, w_ref[:, D_MODEL + lo:D_MODEL + hi]) + b_ref[:, D_MODEL + lo:D_MODEL + hi]
        o_ref[:, lo:hi] = a1 * jax.nn.sigmoid(a2)


def conv_in(h, g, w, b, tm):
    m = h.shape[0]
    return pl.pallas_call(
        _conv_in_body,
        out_shape=jax.ShapeDtypeStruct((m, D_MODEL), F32),
        grid=(m // tm,),
        in_specs=[
            pl.BlockSpec((tm, D_MODEL), lambda i: (i, 0)),
            pl.BlockSpec((1, D_MODEL), lambda i: (0, 0)),
            pl.BlockSpec((D_MODEL, 2 * D_MODEL), lambda i: (0, 0)),
            pl.BlockSpec((1, 2 * D_MODEL), lambda i: (0, 0)),
        ],
        out_specs=pl.BlockSpec((tm, D_MODEL), lambda i: (i, 0)),
        compiler_params=_cparams("parallel"),
        name="conv_in",
    )(h, g, w, b)


HALO = 32
DW_ROWS = 32
DW_LANES = 512


def _ln_silu(y, ln_g, ln_b):
    mu = jnp.mean(y, axis=-1, keepdims=True)
    yc = y - mu
    var = jnp.mean(yc * yc, axis=-1, keepdims=True)
    yn = yc * lax.rsqrt(var + EPS) * ln_g + ln_b
    return yn * jax.nn.sigmoid(yn)


def _dwconv_prompt_body(ts, cur_ref, halo_ref, w_ref, bdw_ref, lng_ref, lnb_ref, z_ref, xin_ref, y_ref):
    first = pl.program_id(1) == 0

    @pl.when(first)
    def _():
        xin_ref[0:HALO, :] = jnp.zeros((HALO, D_MODEL), F32)

    @pl.when(jnp.logical_not(first))
    def _():
        xin_ref[0:HALO, :] = halo_ref[...]

    xin_ref[HALO:HALO + ts, :] = cur_ref[...]
    off = HALO - (CONV_WIDTH - 1)
    for rc in range(ts // DW_ROWS):
        for lc in range(D_MODEL // DW_LANES):
            l0, l1 = lc * DW_LANES, (lc + 1) * DW_LANES
            acc = jnp.zeros((DW_ROWS, DW_LANES), F32)
            for k in range(CONV_WIDTH):
                r0 = off + rc * DW_ROWS + k
                acc = acc + xin_ref[r0:r0 + DW_ROWS, l0:l1] * w_ref[k:k + 1, l0:l1]
            y_ref[rc * DW_ROWS:(rc + 1) * DW_ROWS, l0:l1] = acc
    y = y_ref[...] + bdw_ref[...]
    z_ref[...] = _ln_silu(y, lng_ref[...], lnb_ref[...]).astype(BF16)


def dwconv_prompt(glu, n_seq, seq, w_dw, b_dw, ln_g, ln_b, ts=256):
    nt = seq // ts
    hb = ts // HALO
    return pl.pallas_call(
        functools.partial(_dwconv_prompt_body, ts),
        out_shape=jax.ShapeDtypeStruct((n_seq * seq, D_MODEL), BF16),
        grid=(n_seq, nt),
        in_specs=[
            pl.BlockSpec((ts, D_MODEL), lambda b, i: (b * nt + i, 0)),
            pl.BlockSpec((HALO, D_MODEL), lambda b, i: (jnp.maximum((b * nt + i) * hb - 1, 0), 0)),
            pl.BlockSpec((CONV_WIDTH, D_MODEL), lambda b, i: (0, 0)),
            pl.BlockSpec((1, D_MODEL), lambda b, i: (0, 0)),
            pl.BlockSpec((1, D_MODEL), lambda b, i: (0, 0)),
            pl.BlockSpec((1, D_MODEL), lambda b, i: (0, 0)),
        ],
        out_specs=pl.BlockSpec((ts, D_MODEL), lambda b, i: (b * nt + i, 0)),
        scratch_shapes=[pltpu.VMEM((HALO + ts, D_MODEL), F32), pltpu.VMEM((ts, D_MODEL), F32)],
        compiler_params=_cparams("parallel", "parallel"),
        name="dwconv_prompt",
    )(glu, glu, w_dw, b_dw, ln_g, ln_b)


def _dwconv_sample_body(n_new, st_ref, glu_ref, w_ref, bdw_ref, lng_ref, lnb_ref, z_ref, ns_ref):
    n_st = CONV_WIDTH - 1
    rows = [st_ref[:, j, :] for j in range(n_st)] + [glu_ref[:, j, :] for j in range(n_new)]
    for t in range(n_new):
        acc = rows[t] * w_ref[0:1, :]
        for k in range(1, CONV_WIDTH):
            acc = acc + rows[t + k] * w_ref[k:k + 1, :]
        y = acc + bdw_ref[...]
        z_ref[:, t, :] = _ln_silu(y, lng_ref[...], lnb_ref[...])
    for j in range(n_st):
        ns_ref[:, j, :] = rows[j + n_new]


def dwconv_sample(state, glu, w_dw, b_dw, ln_g, ln_b, bb=8):
    nb, n_new, _ = glu.shape
    n_st = CONV_WIDTH - 1
    vec = pl.BlockSpec((1, D_MODEL), lambda i: (0, 0))
    return pl.pallas_call(
        functools.partial(_dwconv_sample_body, n_new),
        out_shape=(jax.ShapeDtypeStruct((nb, n_new, D_MODEL), F32),
                   jax.ShapeDtypeStruct((nb, n_st, D_MODEL), F32)),
        grid=(nb // bb,),
        in_specs=[
            pl.BlockSpec((bb, n_st, D_MODEL), lambda i: (i, 0, 0)),
            pl.BlockSpec((bb, n_new, D_MODEL), lambda i: (i, 0, 0)),
            pl.BlockSpec((CONV_WIDTH, D_MODEL), lambda i: (0, 0)),
            vec, vec, vec,
        ],
        out_specs=(pl.BlockSpec((bb, n_new, D_MODEL), lambda i: (i, 0, 0)),
                   pl.BlockSpec((bb, n_st, D_MODEL), lambda i: (i, 0, 0))),
        compiler_params=_cparams("parallel"),
        name="dwconv_sample",
    )(state, glu, w_dw, b_dw, ln_g, ln_b)


def _out_proj_body(z_ref, w_ref, b_ref, g_ref, h_ref, o_ref):
    m = _dot(z_ref[...].astype(BF16), w_ref[...]) + b_ref[...]
    o_ref[...] = h_ref[...] + _rms(m, g_ref[...])


def out_proj_res(z, w, b, g, h, tm):
    m = h.shape[0]
    vec = pl.BlockSpec((1, D_MODEL), lambda i: (0, 0))
    row = pl.BlockSpec((tm, D_MODEL), lambda i: (i, 0))
    return pl.pallas_call(
        _out_proj_body,
        out_shape=jax.ShapeDtypeStruct((m, D_MODEL), F32),
        grid=(m // tm,),
        in_specs=[row, pl.BlockSpec((D_MODEL, D_MODEL), lambda i: (0, 0)), vec, vec, row],
        out_specs=row,
        compiler_params=_cparams("parallel"),
        name="out_proj_res",
    )(z, w, b, g, h)


FF_CHUNK = 1024


def _ffn_body(h_ref, p_ref, gpre_ref, gpost_ref, win_ref, wout_ref, wpg_ref, wpp_ref, o_ref, u_ref, acc_ref):
    j = pl.program_id(1)

    @pl.when(j == 0)
    def _():
        u_ref[...] = _rms(h_ref[...], gpre_ref[...]).astype(BF16)
        acc_ref[...] = jnp.zeros_like(acc_ref)

    a = _dot(u_ref[...], win_ref[...])
    a = jnp.square(jnp.maximum(a, 0.0)).astype(BF16)
    acc_ref[...] += _dot(a, wout_ref[...])

    @pl.when(j == pl.num_programs(1) - 1)
    def _():
        h2 = h_ref[...] + _rms(acc_ref[...], gpost_ref[...])
        gate = jax.nn.sigmoid(_dot(h2.astype(BF16), wpg_ref[...]))
        proj = _dot(p_ref[...].astype(BF16), wpp_ref[...])
        o_ref[...] = h2 + gate * proj


def ffn_ple(h, p, g_pre, g_post, w_in, w_out, w_pg, w_pp, tm):
    m = h.shape[0]
    ple = p.shape[1]
    vec = pl.BlockSpec((1, D_MODEL), lambda i, j: (0, 0))
    row = pl.BlockSpec((tm, D_MODEL), lambda i, j: (i, 0))
    return pl.pallas_call(
        _ffn_body,
        out_shape=jax.ShapeDtypeStruct((m, D_MODEL), F32),
        grid=(m // tm, D_FF // FF_CHUNK),
        in_specs=[
            row,
            pl.BlockSpec((tm, ple), lambda i, j: (i, 0)),
            vec, vec,
            pl.BlockSpec((D_MODEL, FF_CHUNK), lambda i, j: (0, j)),
            pl.BlockSpec((FF_CHUNK, D_MODEL), lambda i, j: (j, 0)),
            pl.BlockSpec((D_MODEL, D_MODEL), lambda i, j: (0, 0)),
            pl.BlockSpec((ple, D_MODEL), lambda i, j: (0, 0)),
        ],
        out_specs=row,
        scratch_shapes=[pltpu.VMEM((tm, D_MODEL), BF16), pltpu.VMEM((tm, D_MODEL), F32)],
        compiler_params=_cparams("parallel", "arbitrary"),
        name="ffn_ple",
    )(h, p, g_pre, g_post, w_in, w_out, w_pg, w_pp)


NSA_W_COLS = Q_COLS + KV_COLS + GATE_PAD


def _nsa_in_body(x_ref, g_ref, w_ref, bg_ref, cos_ref, sin_ref,
                 qq_ref, kc_ref, vc_ref, ks_ref, vs_ref, kw_ref, vw_ref, kvb_ref, gate_ref):
    u = _rms(x_ref[...], g_ref[...]).astype(BF16)
    tm = u.shape[0]
    cos = jnp.concatenate([cos_ref[...], cos_ref[...]], axis=1)
    sin = jnp.concatenate([sin_ref[...], sin_ref[...]], axis=1)
    lane = lax.broadcasted_iota(jnp.int32, (tm, KV_W), 1)
    first_half = (lane & (HEAD_DIM - 1)) < (HEAD_DIM // 2)

    def rope(x):
        rot = jnp.where(first_half, pltpu.roll(x, KV_W - HEAD_DIM // 2, 1), pltpu.roll(x, HEAD_DIM // 2, 1))
        return x * cos + rot * sin

    scale = HEAD_DIM ** -0.5
    for j in range(Q_COLS // KV_W):
        q = _dot(u, w_ref[:, j * KV_W:(j + 1) * KV_W]) * scale
        qq_ref[:, j * KV_W:(j + 1) * KV_W] = q.astype(BF16)
        qq_ref[:, Q_COLS + j * KV_W:Q_COLS + (j + 1) * KV_W] = rope(q).astype(BF16)
    outs = (kc_ref, vc_ref, ks_ref, vs_ref, kw_ref, vw_ref)
    for j in range(2 * N_BRANCH):
        x = _dot(u, w_ref[:, Q_COLS + j * KV_W:Q_COLS + (j + 1) * KV_W])
        if j in (2, 4):
            x = rope(x)
        outs[j][...] = x
        if j >= 2:
            kvb_ref[:, (j - 2) * KV_W:(j - 1) * KV_W] = x.astype(BF16)
    gl = _dot(u, w_ref[:, Q_COLS + KV_COLS:]) + bg_ref[...]
    gate_ref[...] = jax.nn.sigmoid(gl)


def nsa_in(h, g, w, bg, cos_t, sin_t, pos_map, tm):
    m = h.shape[0]
    kv_spec = pl.BlockSpec((tm, KV_W), lambda i: (i, 0))
    kv_shape = jax.ShapeDtypeStruct((m, KV_W), F32)
    return pl.pallas_call(
        _nsa_in_body,
        out_shape=(jax.ShapeDtypeStruct((m, 2 * Q_COLS), BF16),
                   kv_shape, kv_shape, kv_shape, kv_shape, kv_shape, kv_shape,
                   jax.ShapeDtypeStruct((m, 4 * KV_W), BF16),
                   jax.ShapeDtypeStruct((m, GATE_PAD), F32)),
        grid=(m // tm,),
        in_specs=[
            pl.BlockSpec((tm, D_MODEL), lambda i: (i, 0)),
            pl.BlockSpec((1, D_MODEL), lambda i: (0, 0)),
            pl.BlockSpec((D_MODEL, NSA_W_COLS), lambda i: (0, 0)),
            pl.BlockSpec((1, GATE_PAD), lambda i: (0, 0)),
            pl.BlockSpec((tm, 2 * HEAD_DIM), lambda i: (pos_map(i), 0)),
            pl.BlockSpec((tm, 2 * HEAD_DIM), lambda i: (pos_map(i), 0)),
        ],
        out_specs=(pl.BlockSpec((tm, 2 * Q_COLS), lambda i: (i, 0)),
                   kv_spec, kv_spec, kv_spec, kv_spec, kv_spec, kv_spec,
                   pl.BlockSpec((tm, 4 * KV_W), lambda i: (i, 0)),
                   pl.BlockSpec((tm, GATE_PAD), lambda i: (i, 0))),
        compiler_params=_cparams("parallel"),
        name="nsa_in",
    )(h, g, w, bg, cos_t, sin_t)


CHUNK_W = CMP_STRIDE * HEAD_DIM
TOK_PAIRS = CMP_STRIDE // 2


def _compress_one(load_t, nc, n_out, w1_ref, pe_ref, w2_ref, out_ref, c_ref, ab_ref):
    lane = lax.broadcasted_iota(jnp.int32, (nc, 2 * HEAD_DIM), 1)
    low = lane < HEAD_DIM
    for tp in range(TOK_PAIRS):
        for half in range(KV_GROUPS // 2):
            a = load_t(2 * tp, half)
            b = load_t(2 * tp + 1, half)
            even = jnp.where(low, a, pltpu.roll(b, HEAD_DIM, 1))
            odd = jnp.where(low, pltpu.roll(a, HEAD_DIM, 1), b)
            g0 = 2 * half
            c_ref[g0 * nc:(g0 + 1) * nc, tp * 128:(tp + 1) * 128] = even.astype(BF16)
            c_ref[(g0 + 1) * nc:(g0 + 2) * nc, tp * 128:(tp + 1) * 128] = odd.astype(BF16)
    ab_ref[0:KV_GROUPS * nc, :] = _dot(c_ref[...], w1_ref[...])
    ab_ref[KV_GROUPS * nc:KV_GROUPS * nc + 8, :] = jnp.zeros((8, 2 * CMP_HIDDEN), F32)
    pe2 = _dot(pe_ref[...].astype(BF16), w1_ref[...])
    pe_c = pe2[0:1, 0:CMP_HIDDEN] + pe2[1:2, CMP_HIDDEN:2 * CMP_HIDDEN]
    hs = []
    for g in range(KV_GROUPS):
        top = ab_ref[g * nc:g * nc + n_out, 0:CMP_HIDDEN]
        bot = ab_ref[g * nc + 1:g * nc + 1 + n_out, CMP_HIDDEN:2 * CMP_HIDDEN]
        hs.append(jax.nn.gelu(top + bot + pe_c).astype(BF16))
    hcat = jnp.concatenate(hs, axis=1)
    out_ref[0] = _dot(hcat, w2_ref[...]).astype(BF16)


def _compress_prompt_body(nc, k0_ref, k1_ref, v0_ref, v1_ref, wk1_ref, pek_ref, wk2_ref, wv1_ref, pev_ref, wv2_ref,
                          ck_ref, cv_ref, c_ref, ab_ref):
    def loader(halves):
        return lambda t, half: halves[half][pl.ds(t, nc, stride=CMP_STRIDE), :]

    _compress_one(loader((k0_ref, k1_ref)), nc, nc, wk1_ref, pek_ref, wk2_ref, ck_ref, c_ref, ab_ref)
    _compress_one(loader((v0_ref, v1_ref)), nc, nc, wv1_ref, pev_ref, wv2_ref, cv_ref, c_ref, ab_ref)


def _compress_weight_specs(idx):
    return [pl.BlockSpec((CHUNK_W, 2 * CMP_HIDDEN), idx),
            pl.BlockSpec((8, CHUNK_W), idx),
            pl.BlockSpec((KV_GROUPS * CMP_HIDDEN, KV_W), idx)]


def compress_prompt(kc, vc, n_seq, seq, wk, wv):
    nc = seq // CMP_STRIDE
    idx = lambda b: (0, 0)
    out = jax.ShapeDtypeStruct((n_seq, nc, KV_W), BF16)
    halves = [pl.BlockSpec((seq, 128), lambda b: (b, 0)), pl.BlockSpec((seq, 128), lambda b: (b, 1))]
    return pl.pallas_call(
        functools.partial(_compress_prompt_body, nc),
        out_shape=(out, out),
        grid=(n_seq,),
        in_specs=halves + halves + _compress_weight_specs(idx) + _compress_weight_specs(idx),
        out_specs=(pl.BlockSpec((1, nc, KV_W), lambda b: (b, 0, 0)), pl.BlockSpec((1, nc, KV_W), lambda b: (b, 0, 0))),
        scratch_shapes=[pltpu.VMEM((KV_GROUPS * nc, CHUNK_W), BF16),
                        pltpu.VMEM((KV_GROUPS * nc + 8, 2 * CMP_HIDDEN), F32)],
        compiler_params=_cparams("parallel"),
        name="compress_prompt",
    )(kc, kc, vc, vc, *wk, *wv)


def _compress_sample_body(n_pages, page, nc, n_out, pt_ref, *refs):
    pages = [refs[i * n_pages:(i + 1) * n_pages] for i in range(4)]
    k_new, v_new = refs[4 * n_pages:4 * n_pages + 2]
    wk1_ref, pek_ref, wk2_ref, wv1_ref, pev_ref, wv2_ref = refs[4 * n_pages + 2:4 * n_pages + 8]
    ck_ref, cv_ref, c_ref, ab_ref = refs[4 * n_pages + 8:]
    per_page = page // CMP_STRIDE
    n_tail = nc - n_pages * per_page - 8
    row0 = lax.broadcasted_iota(jnp.int32, (8, 128), 0) == 0

    def loader(page_halves, new_ref):
        def load_t(t, half):
            parts = [p[0, pl.ds(t, per_page, stride=CMP_STRIDE), :] for p in page_halves[half]]
            new_row = jnp.broadcast_to(new_ref[0, t:t + 1, half * 128:(half + 1) * 128], (8, 128))
            parts.append(jnp.where(row0, new_row, 0.0))
            if n_tail:
                parts.append(jnp.zeros((n_tail, 128), F32))
            return jnp.concatenate(parts, axis=0)
        return load_t

    _compress_one(loader(pages[0:2], k_new), nc, n_out, wk1_ref, pek_ref, wk2_ref, ck_ref, c_ref, ab_ref)
    _compress_one(loader(pages[2:4], v_new), nc, n_out, wv1_ref, pev_ref, wv2_ref, cv_ref, c_ref, ab_ref)


def compress_sample(pool_k, pool_v, page_table, k_new, v_new, wk, wv):
    nb, n_pages = page_table.shape
    page = pool_k.shape[1]
    n_chunk = n_pages * (page // CMP_STRIDE) + 1
    n_out = n_chunk - 1
    nc = -(-(n_chunk + 1) // 16) * 16
    idx = lambda b, pt: (0, 0)

    def page_specs(half):
        return [pl.BlockSpec((1, page, 128), functools.partial(lambda p, b, pt: (pt[b * n_pages + p], 0, half), p))
                for p in range(n_pages)]

    new_spec = pl.BlockSpec((1, CMP_STRIDE, KV_W), lambda b, pt: (b, 0, 0))
    out = jax.ShapeDtypeStruct((nb, n_out, KV_W), BF16)
    grid_spec = pltpu.PrefetchScalarGridSpec(
        num_scalar_prefetch=1,
        grid=(nb,),
        in_specs=(page_specs(0) + page_specs(1)) * 2 + [new_spec, new_spec]
        + _compress_weight_specs(idx) + _compress_weight_specs(idx),
        out_specs=(pl.BlockSpec((1, n_out, KV_W), lambda b, pt: (b, 0, 0)),
                   pl.BlockSpec((1, n_out, KV_W), lambda b, pt: (b, 0, 0))),
        scratch_shapes=[pltpu.VMEM((KV_GROUPS * nc, CHUNK_W), BF16),
                        pltpu.VMEM((KV_GROUPS * nc + 8, 2 * CMP_HIDDEN), F32)],
    )
    return pl.pallas_call(
        functools.partial(_compress_sample_body, n_pages, page, nc, n_out),
        out_shape=(out, out),
        grid_spec=grid_spec,
        compiler_params=_cparams("parallel"),
        name="compress_sample",
    )(page_table.reshape(-1), *([pool_k] * (2 * n_pages)), *([pool_v] * (2 * n_pages)), k_new, v_new, *wk, *wv)


def _group_masks():
    lane = lax.broadcasted_iota(jnp.int32, (1, KV_W), 1)
    return [(lane // HEAD_DIM) == g for g in range(KV_GROUPS)]


def _stack_q(q, gmask):
    if q.shape[0] % 16:
        q = q.astype(F32)
    zero = jnp.zeros((q.shape[0], KV_W), q.dtype)
    return jnp.concatenate([jnp.where(gmask[g], q[:, r * KV_W:(r + 1) * KV_W], zero)
                            for g in range(KV_GROUPS) for r in range(GROUP_SIZE)], axis=0).astype(BF16)


def _unstack_o(o, tq, gmask):
    cols = []
    for r in range(GROUP_SIZE):
        acc = jnp.zeros((tq, KV_W), F32)
        for g in range(KV_GROUPS):
            row = (g * GROUP_SIZE + r) * tq
            acc = acc + jnp.where(gmask[g], o[row:row + tq, :], 0.0)
        cols.append(acc)
    return jnp.concatenate(cols, axis=1)


def _rep_rows(x, tq):
    return jnp.concatenate([x[g * tq:(g + 1) * tq] for g in range(KV_GROUPS) for _ in range(GROUP_SIZE)], axis=0)


def _softmax_rows(s, mask):
    s = jnp.where(mask, s, NEG)
    m = jnp.max(s, axis=1, keepdims=True)
    e = jnp.where(mask, jnp.exp(s - m), 0.0)
    d = jnp.sum(e, axis=1, keepdims=True)
    return e * (1.0 / jnp.where(d > 0, d, 1.0))


def _cmp_branch(qc, ck, cv, qpos0, tq):
    rows, n = qc.shape[0], ck.shape[0]
    s = _dot_t(qc, ck)
    col = lax.broadcasted_iota(jnp.int32, (rows, n), 1)
    qp = qpos0 + (lax.broadcasted_iota(jnp.int32, (rows, n), 0) & (tq - 1))
    p = _softmax_rows(s, col * CMP_STRIDE + (CMP_BLOCK - 1) <= qp)
    o = _dot(p.astype(BF16), cv)
    psum = []
    for g in range(KV_GROUPS):
        acc = p[g * GROUP_SIZE * tq:(g * GROUP_SIZE + 1) * tq]
        for r in range(1, GROUP_SIZE):
            acc = acc + p[(g * GROUP_SIZE + r) * tq:(g * GROUP_SIZE + r + 1) * tq]
        psum.append(acc)
    return o, jnp.concatenate(psum, axis=0)


def _select_bias(psum, cov_t, qpos0, tq):
    hi, lo = _split_bf16(psum)
    imp = _dot_t(cov_t, hi) + _dot_t(cov_t, lo)
    shape = imp.shape
    blk = lax.broadcasted_iota(jnp.int32, shape, 0)
    cur = (qpos0 + (lax.broadcasted_iota(jnp.int32, shape, 1) & (tq - 1))) // SEL_BLOCK
    forced = (blk == 0) | (blk == cur) | (blk == cur - 1)
    score = jnp.where(blk > cur, -jnp.inf, jnp.where(forced, jnp.inf, imp))
    rank = jnp.zeros(shape, F32)
    for k in range(MAX_SEL_BLOCKS):
        sk = jnp.broadcast_to(score[k:k + 1, :], shape)
        ge = jnp.where(sk >= score, 1.0, 0.0)
        gt = jnp.where(sk > score, 1.0, 0.0)
        rank = rank + jnp.where(blk > k, ge, gt)
    bias_t = jnp.where(rank < N_SELECT, 0.0, NEG)
    return jnp.transpose(bias_t).astype(BF16)


def _gate_mix(gate, gexp_ref, outs):
    hi, lo = _split_bf16(gate)
    total = None
    for br, o in enumerate(outs):
        ge = _dot(hi, gexp_ref[br]) + _dot(lo, gexp_ref[br])
        total = ge * o if total is None else total + ge * o
    return total


TQ_P = 64
KEY_CHUNK = 512


def _attn_prompt_body(seq, qq_ref, gate_ref, ck_ref, cv_ref, ks_ref, vs_ref, kw_ref, vw_ref,
                      cov_ref, et_ref, gexp_ref, o_ref):
    tq = TQ_P
    rows = KV_GROUPS * GROUP_SIZE * tq
    n = pl.program_id(1)
    q0 = n * tq
    gmask = _group_masks()
    qc = _stack_q(qq_ref[:, 0:Q_COLS], gmask)
    qr = _stack_q(qq_ref[:, Q_COLS:2 * Q_COLS], gmask)

    o_c, psum = _cmp_branch(qc, ck_ref[0], cv_ref[0], q0, tq)
    bias = _select_bias(psum, cov_ref[...], q0, tq)

    row_q = q0 + (lax.broadcasted_iota(jnp.int32, (rows, KEY_CHUNK), 0) & (tq - 1))
    col_k = lax.broadcasted_iota(jnp.int32, (rows, KEY_CHUNK), 1)

    def chunk(c, carry):
        m, l, acc = carry
        k0 = pl.multiple_of(c * KEY_CHUNK, KEY_CHUNK)
        s = _dot_t(qr, ks_ref[pl.ds(k0, KEY_CHUNK), :])
        b = _rep_rows(_dot_t(bias, et_ref[pl.ds(k0, KEY_CHUNK), :]), tq)
        s = jnp.where(col_k + k0 <= row_q, s + b, NEG)
        m_new = jnp.maximum(m, jnp.max(s, axis=1, keepdims=True))
        alpha = jnp.exp(m - m_new)
        p = jnp.exp(s - m_new)
        l = alpha * l + jnp.sum(p, axis=1, keepdims=True)
        acc = alpha * acc + _dot(p.astype(BF16), vs_ref[pl.ds(k0, KEY_CHUNK), :])
        return m_new, l, acc

    n_chunks = (q0 + tq + KEY_CHUNK - 1) // KEY_CHUNK
    m0 = jnp.full((rows, 1), NEG, F32)
    l0 = jnp.zeros((rows, 1), F32)
    a0 = jnp.zeros((rows, KV_W), F32)
    _, l, acc = lax.fori_loop(0, n_chunks, chunk, (m0, l0, a0))
    o_s = acc * (1.0 / l)

    nk = WINDOW + tq
    base = pl.multiple_of(jnp.maximum(q0 - WINDOW, 0), tq)
    s = _dot_t(qr, kw_ref[pl.ds(base, nk), :])
    kpos = base + lax.broadcasted_iota(jnp.int32, (rows, nk), 1)
    qp = q0 + (lax.broadcasted_iota(jnp.int32, (rows, nk), 0) & (tq - 1))
    p = _softmax_rows(s, (kpos <= qp) & (kpos > qp - WINDOW))
    o_w = _dot(p.astype(BF16), vw_ref[pl.ds(base, nk), :])

    outs = [_unstack_o(o, tq, gmask) for o in (o_c, o_s, o_w)]
    o_ref[...] = _gate_mix(gate_ref[...], gexp_ref, outs).astype(BF16)


def attn_prompt(qq, gate, ck, cv, kvb, n_seq, seq, cov_t, e_t, gexp):
    tq = TQ_P
    nt = seq // tq
    n_cmp = ck.shape[1]

    def kv_spec(col):
        return pl.BlockSpec((seq, KV_W), lambda b, i: (b, col))

    return pl.pallas_call(
        functools.partial(_attn_prompt_body, seq),
        out_shape=jax.ShapeDtypeStruct((n_seq * seq, Q_COLS), BF16),
        grid=(n_seq, nt),
        in_specs=[
            pl.BlockSpec((tq, 2 * Q_COLS), lambda b, i: (b * nt + i, 0)),
            pl.BlockSpec((tq, GATE_PAD), lambda b, i: (b * nt + i, 0)),
            pl.BlockSpec((1, n_cmp, KV_W), lambda b, i: (b, 0, 0)),
            pl.BlockSpec((1, n_cmp, KV_W), lambda b, i: (b, 0, 0)),
            kv_spec(0), kv_spec(1), kv_spec(2), kv_spec(3),
            pl.BlockSpec(cov_t.shape, lambda b, i: (0, 0)),
            pl.BlockSpec(e_t.shape, lambda b, i: (0, 0)),
            pl.BlockSpec(gexp.shape, lambda b, i: (0, 0, 0)),
        ],
        out_specs=pl.BlockSpec((tq, Q_COLS), lambda b, i: (b * nt + i, 0)),
        compiler_params=_cparams("parallel", "parallel"),
        name="attn_prompt",
    )(qq, gate, ck, cv, kvb, kvb, kvb, kvb, cov_t, e_t, gexp)


TQ_S = 8
NEW_PAD = 128


def _attn_sample_body(n_pages, page, n_new, pt_ref, *refs):
    ks_pages = refs[0:n_pages]
    vs_pages = refs[n_pages:2 * n_pages]
    (qq_ref, gate_ref, ck_ref, cv_ref, kvn_ref, kwn_ref, vwn_ref, kwb_ref, vwb_ref,
     cov_ref, et_ref, gexp_ref, o_ref, kwo_ref, vwo_ref) = refs[2 * n_pages:]
    tq = TQ_S
    rows = KV_GROUPS * GROUP_SIZE * tq
    past = n_pages * page
    gmask = _group_masks()
    qc = _stack_q(qq_ref[0, :, 0:Q_COLS], gmask)
    qr = _stack_q(qq_ref[0, :, Q_COLS:2 * Q_COLS], gmask)

    o_c, psum = _cmp_branch(qc, ck_ref[0], cv_ref[0], past, tq)
    bias = _select_bias(psum, cov_ref[...], past, tq)

    pad = jnp.zeros((NEW_PAD - tq, KV_W), BF16)

    def new_rows(col):
        return jnp.concatenate([kvn_ref[0, :, col * KV_W:(col + 1) * KV_W], pad], axis=0)

    k_all = jnp.concatenate([p[0].astype(BF16) for p in ks_pages] + [new_rows(0)], axis=0)
    v_all = jnp.concatenate([p[0].astype(BF16) for p in vs_pages] + [new_rows(1)], axis=0)
    nk = past + NEW_PAD
    s = _dot_t(qr, k_all) + _rep_rows(_dot_t(bias, et_ref[...]), tq)
    kpos = lax.broadcasted_iota(jnp.int32, (rows, nk), 1)
    qp = past + (lax.broadcasted_iota(jnp.int32, (rows, nk), 0) & (tq - 1))
    o_s = _dot(_softmax_rows(s, kpos <= qp).astype(BF16), v_all)

    wb = kwb_ref.shape[1]
    kw_all = jnp.concatenate([kwb_ref[0].astype(BF16), new_rows(2)], axis=0)
    vw_all = jnp.concatenate([vwb_ref[0].astype(BF16), new_rows(3)], axis=0)
    nw = wb + NEW_PAD
    s = _dot_t(qr, kw_all)
    kpos = (past - wb) + lax.broadcasted_iota(jnp.int32, (rows, nw), 1)
    qp = past + (lax.broadcasted_iota(jnp.int32, (rows, nw), 0) & (tq - 1))
    mask = (kpos >= 0) & (kpos <= qp) & (kpos > qp - WINDOW)
    o_w = _dot(_softmax_rows(s, mask).astype(BF16), vw_all)

    outs = [_unstack_o(o, tq, gmask) for o in (o_c, o_s, o_w)]
    o_ref[0] = _gate_mix(gate_ref[0], gexp_ref, outs).astype(BF16)

    keep = wb - n_new
    kwo_ref[0, 0:keep, :] = kwb_ref[0, n_new:wb, :]
    kwo_ref[0, keep:wb, :] = kwn_ref[0, 0:n_new, :]
    vwo_ref[0, 0:keep, :] = vwb_ref[0, n_new:wb, :]
    vwo_ref[0, keep:wb, :] = vwn_ref[0, 0:n_new, :]


def attn_sample(pool_ks, pool_vs, page_table, qq, gate, ck, cv, kvn, kwn, vwn, kwb, vwb, n_new, cov_t, e_t, gexp):
    nb, n_pages = page_table.shape
    page = pool_ks.shape[1]
    wb = kwb.shape[1]
    n_cmp = ck.shape[1]

    def page_spec(p):
        return pl.BlockSpec((1, page, KV_W), lambda b, pt: (pt[b * n_pages + p], 0, 0))

    def bspec(shape):
        return pl.BlockSpec((1,) + shape, lambda b, pt: (b, 0, 0))

    def cspec(shape):
        return pl.BlockSpec(shape, lambda b, pt: (0,) * len(shape))

    grid_spec = pltpu.PrefetchScalarGridSpec(
        num_scalar_prefetch=1,
        grid=(nb,),
        in_specs=[page_spec(p) for p in range(n_pages)] * 2 + [
            bspec((TQ_S, 2 * Q_COLS)), bspec((TQ_S, GATE_PAD)), bspec((n_cmp, KV_W)), bspec((n_cmp, KV_W)),
            bspec((TQ_S, 4 * KV_W)), bspec((TQ_S, KV_W)), bspec((TQ_S, KV_W)),
            bspec((wb, KV_W)), bspec((wb, KV_W)),
            cspec(cov_t.shape), cspec(e_t.shape), cspec(gexp.shape),
        ],
        out_specs=(bspec((TQ_S, Q_COLS)), bspec((wb, KV_W)), bspec((wb, KV_W))),
    )
    return pl.pallas_call(
        functools.partial(_attn_sample_body, n_pages, page, n_new),
        out_shape=(jax.ShapeDtypeStruct((nb, TQ_S, Q_COLS), BF16),
                   jax.ShapeDtypeStruct((nb, wb, KV_W), F32),
                   jax.ShapeDtypeStruct((nb, wb, KV_W), F32)),
        grid_spec=grid_spec,
        compiler_params=_cparams("parallel"),
        name="attn_sample",
    )(page_table.reshape(-1), *([pool_ks] * n_pages), *([pool_vs] * n_pages),
      qq, gate, ck, cv, kvn, kwn, vwn, kwb, vwb, cov_t, e_t, gexp)


def _rope_tables(pos):
    half = HEAD_DIM // 2
    inv = ROPE_THETA ** (-jnp.arange(half, dtype=F32) / half)
    ang = pos.astype(F32)[:, None] * inv[None, :]
    c, s = jnp.cos(ang), jnp.sin(ang)
    return jnp.tile(jnp.concatenate([c, c], axis=1), (1, 2)), jnp.tile(jnp.concatenate([-s, s], axis=1), (1, 2))


def _head_perm():
    c = np.arange(Q_COLS)
    r, g, d = c // KV_W, (c % KV_W) // HEAD_DIM, c % HEAD_DIM
    return (g * GROUP_SIZE + r) * HEAD_DIM + d


def _cover_t(n_cmp, n_cmp_pad):
    cs = np.arange(n_cmp_pad) * CMP_STRIDE
    ss = np.arange(MAX_SEL_BLOCKS) * SEL_BLOCK
    m = (cs[None, :] < ss[:, None] + SEL_BLOCK) & (cs[None, :] + CMP_BLOCK > ss[:, None])
    m = m & (np.arange(n_cmp_pad)[None, :] < n_cmp)
    return jnp.asarray(m, dtype=BF16)


def _block_onehot_t(n_keys):
    k = np.arange(n_keys)
    return jnp.asarray((k[:, None] // SEL_BLOCK) == np.arange(MAX_SEL_BLOCKS)[None, :], dtype=BF16)


def _gate_expand():
    ge = np.zeros((N_BRANCH, GATE_PAD, Q_COLS), np.float32)
    c = np.arange(Q_COLS)
    r, g = c // KV_W, (c % KV_W) // HEAD_DIM
    for br in range(N_BRANCH):
        ge[br, (g * GROUP_SIZE + r) * N_BRANCH + br, c] = 1.0
    return jnp.asarray(ge, dtype=BF16)


def _compress_weights(pe, w1, w2):
    half = CMP_BLOCK // 2 * HEAD_DIM
    w1cat = jnp.concatenate([w1[:half], w1[half:]], axis=1).astype(BF16)
    pe2 = jnp.pad(pe.reshape(2, half), ((0, 6), (0, 0)))
    w2bd = jnp.kron(jnp.eye(KV_GROUPS, dtype=F32), w2).astype(BF16)
    return w1cat, pe2, w2bd


def _row(v):
    return v.reshape(1, -1)


TM_PROMPT = 512
TM_SAMPLE = 512


def kernel(x_prompt, x_sample, state_conv, cache_k_cmp, cache_v_cmp, cache_k_sel, cache_v_sel, state_k_win, state_v_win, page_table, p_prompt, p_sample, norm_mix_pre, norm_mix_post, norm_ffn_pre, norm_ffn_post, conv_w_in, conv_b_in, conv_w_dw, conv_b_dw, conv_ln_g, conv_ln_b, conv_w_out, conv_b_out, nsa_w_in, nsa_b_gate, nsa_pe_k, nsa_pe_v, nsa_wk_c1, nsa_wk_c2, nsa_wv_c1, nsa_wv_c2, nsa_w_o, ffn_w_in, ffn_w_out, ple_w_gate, ple_w_proj):
    n_p, seq, _ = x_prompt.shape
    n_s, n_new, _ = x_sample.shape
    n_pages = page_table.shape[1]
    page = cache_k_cmp.shape[2]
    past = n_pages * page
    tp, ts = n_p * seq, n_s * n_new
    assert seq % KEY_CHUNK == 0 and seq // SEL_BLOCK <= MAX_SEL_BLOCKS and seq >= WINDOW + TQ_P
    assert (past + n_new + SEL_BLOCK - 1) // SEL_BLOCK <= MAX_SEL_BLOCKS and n_new <= TQ_S
    assert past % SEL_BLOCK == 0 and page % CMP_STRIDE == 0 and state_k_win.shape[2] == WINDOW

    hp = x_prompt.reshape(tp, D_MODEL)
    hs = x_sample.reshape(ts, D_MODEL)
    pp = p_prompt.reshape(DEPTH, tp, -1)
    ps = p_sample.reshape(DEPTH, ts, -1)

    perm = _head_perm()
    gexp = _gate_expand()
    cos_p, sin_p = _rope_tables(jnp.arange(seq))
    cos_s, sin_s = _rope_tables(past + jnp.arange(ts) % n_new)
    n_cmp_p = seq // CMP_STRIDE - 1
    cov_p = _cover_t(n_cmp_p, seq // CMP_STRIDE)
    n_cmp_s = past // CMP_STRIDE
    cov_s = _cover_t(n_cmp_s, n_cmp_s)
    et_p = _block_onehot_t(seq)
    et_s = _block_onehot_t(past + NEW_PAD)
    tiles_per_seq = seq // TM_PROMPT

    def pad_rows(x, n):
        return jnp.pad(x, ((0, 0), (0, n - x.shape[1]), (0, 0)))

    conv_p, conv_s, nsa_p, nsa_s = [], [], [], []
    for i in range(DEPTH):
        j = i // N_MIXERS
        g_pre, g_post = _row(norm_mix_pre[i]), _row(norm_mix_post[i])
        if i % N_MIXERS == 0:
            w_in = conv_w_in[j].astype(BF16)
            b_in = _row(conv_b_in[j])
            w_out = conv_w_out[j].astype(BF16)
            dw = (conv_w_dw[j], _row(conv_b_dw[j]), _row(conv_ln_g[j]), _row(conv_ln_b[j]))
            glu_p = conv_in(hp, g_pre, w_in, b_in, TM_PROMPT)
            glu_s = conv_in(hs, g_pre, w_in, b_in, TM_SAMPLE)
            z_p = dwconv_prompt(glu_p, n_p, seq, *dw)
            z_s, st_s = dwconv_sample(state_conv[j], glu_s.reshape(n_s, n_new, D_MODEL), *dw)
            conv_p.append(glu_p.reshape(n_p, seq, D_MODEL)[:, seq - (CONV_WIDTH - 1):])
            conv_s.append(st_s)
            b_out = _row(conv_b_out[j])
            hp = out_proj_res(z_p, w_out, b_out, g_post, hp, TM_PROMPT)
            hs = out_proj_res(z_s.reshape(ts, D_MODEL), w_out, b_out, g_post, hs, TM_SAMPLE)
        else:
            w = nsa_w_in[j]
            w_in = jnp.concatenate([w[:, :Q_COLS][:, perm], w[:, Q_COLS:],
                                    jnp.zeros((D_MODEL, GATE_PAD - GATE_COLS), F32)], axis=1).astype(BF16)
            bg = jnp.pad(nsa_b_gate[j], (0, GATE_PAD - GATE_COLS)).reshape(1, GATE_PAD)
            w_o = nsa_w_o[j][perm].astype(BF16)
            wk = _compress_weights(nsa_pe_k[j], nsa_wk_c1[j], nsa_wk_c2[j])
            wv = _compress_weights(nsa_pe_v[j], nsa_wv_c1[j], nsa_wv_c2[j])
            zero_b = jnp.zeros((1, D_MODEL), F32)

            qq, kc, vc, ks, vs, kw, vw, kvb, gate = nsa_in(
                hp, g_pre, w_in, bg, cos_p, sin_p, lambda t: t % tiles_per_seq, TM_PROMPT)
            ck, cv = compress_prompt(kc, vc, n_p, seq, wk, wv)
            o_p = attn_prompt(qq, gate, ck, cv, kvb, n_p, seq, cov_p, et_p, gexp)
            hp = out_proj_res(o_p, w_o, zero_b, g_post, hp, TM_PROMPT)
            kvshape = (n_p, seq, KV_GROUPS, HEAD_DIM)
            keep = min(WINDOW, seq)
            nsa_p.append((kc.reshape(kvshape), vc.reshape(kvshape), ks.reshape(kvshape), vs.reshape(kvshape),
                          kw.reshape(kvshape)[:, seq - keep:], vw.reshape(kvshape)[:, seq - keep:]))

            qq, kc, vc, ks, vs, kw, vw, kvb, gate = nsa_in(
                hs, g_pre, w_in, bg, cos_s, sin_s, lambda t: t, TM_SAMPLE)
            r3 = lambda x: x.reshape(n_s, n_new, x.shape[-1])
            pool = lambda c: c[j].reshape(c.shape[1], page, KV_W)
            ck, cv = compress_sample(pool(cache_k_cmp), pool(cache_v_cmp), page_table,
                                     pad_rows(r3(kc), CMP_STRIDE), pad_rows(r3(vc), CMP_STRIDE), wk, wv)
            wbuf = lambda s: s[j].reshape(n_s, s.shape[2], KV_W)
            o_s, kwo, vwo = attn_sample(
                pool(cache_k_sel), pool(cache_v_sel), page_table,
                pad_rows(r3(qq), TQ_S), pad_rows(r3(gate), TQ_S), ck, cv,
                pad_rows(r3(kvb), TQ_S), pad_rows(r3(kw), TQ_S), pad_rows(r3(vw), TQ_S),
                wbuf(state_k_win), wbuf(state_v_win), n_new, cov_s, et_s, gexp)
            hs = out_proj_res(o_s[:, :n_new].reshape(ts, Q_COLS), w_o, zero_b, g_post, hs, TM_SAMPLE)
            kvshape = (n_s, n_new, KV_GROUPS, HEAD_DIM)
            wshape = (n_s, kwo.shape[1], KV_GROUPS, HEAD_DIM)
            nsa_s.append((kc.reshape(kvshape), vc.reshape(kvshape), ks.reshape(kvshape), vs.reshape(kvshape),
                          kwo.reshape(wshape), vwo.reshape(wshape)))

        fw = (_row(norm_ffn_pre[i]), _row(norm_ffn_post[i]), ffn_w_in[i].astype(BF16), ffn_w_out[i].astype(BF16),
              ple_w_gate[i].astype(BF16), ple_w_proj[i].astype(BF16))
        hp = ffn_ple(hp, pp[i], *fw, TM_PROMPT)
        hs = ffn_ple(hs, ps[i], *fw, TM_SAMPLE)

    stack = lambda lst, k: jnp.stack([st[k] for st in lst])
    return (hp.reshape(x_prompt.shape), hs.reshape(x_sample.shape), jnp.stack(conv_p),
            stack(nsa_p, 0), stack(nsa_p, 1), stack(nsa_p, 2), stack(nsa_p, 3), stack(nsa_p, 4), stack(nsa_p, 5),
            jnp.stack(conv_s),
            stack(nsa_s, 0), stack(nsa_s, 1), stack(nsa_s, 2), stack(nsa_s, 3), stack(nsa_s, 4), stack(nsa_s, 5))
```

```python
import functools

import numpy as np
import jax
import jax.numpy as jnp
from jax import lax
from jax.experimental import pallas as pl
from jax.experimental.pallas import tpu as pltpu

F32 = jnp.float32
BF16 = jnp.bfloat16

D_MODEL = 1024
DEPTH = 4
N_MIXERS = 2
CONV_WIDTH = 31
N_HEADS = 16
HEAD_DIM = 64
KV_GROUPS = 4
GROUP_SIZE = 4
N_BRANCH = 3
CMP_BLOCK = 32
CMP_STRIDE = 16
CMP_HIDDEN = 2 * HEAD_DIM
SEL_BLOCK = 64
N_SELECT = 16
WINDOW = 512
D_FF = 4 * D_MODEL
ROPE_THETA = 10000.0
EPS = 1e-6
KV_W = KV_GROUPS * HEAD_DIM
Q_COLS = N_HEADS * HEAD_DIM
KV_COLS = 2 * N_BRANCH * KV_W
GATE_COLS = N_BRANCH * N_HEADS
GATE_PAD = 128
NEG = -1e30
LOG2E = 1.4426950408889634
MAX_SEL_BLOCKS = 64
VMEM_LIMIT = 56 * 1024 * 1024


def _cparams(*sem):
    return pltpu.CompilerParams(dimension_semantics=sem, vmem_limit_bytes=VMEM_LIMIT)


def _rms(x, g):
    return x * lax.rsqrt(jnp.mean(x * x, axis=-1, keepdims=True) + EPS) * g


def _dot(a, b):
    return jnp.dot(a, b, preferred_element_type=F32)


def _dot_t(a, b):
    return lax.dot_general(a, b, (((1,), (1,)), ((), ())), preferred_element_type=F32)


def _split_bf16(x):
    hi = x.astype(BF16)
    lo = (x - hi.astype(F32)).astype(BF16)
    return hi, lo


COL_TILE = 256


def _conv_in_body(x_ref, g_ref, w_ref, b_ref, o_ref):
    u = _rms(x_ref[...], g_ref[...]).astype(BF16)
    for j in range(D_MODEL // COL_TILE):
        lo, hi = j * COL_TILE, (j + 1) * COL_TILE
        a1 = _dot(u, w_ref[:, lo:hi]) + b_ref[:, lo:hi]
        a2 = _dot(u, w_ref[:, D_MODEL + lo:D_MODEL + hi]) + b_ref[:, D_MODEL + lo:D_MODEL + hi]
        o_ref[:, lo:hi] = a1 * jax.nn.sigmoid(a2)


def conv_in(h, g, w, b, tm):
    m = h.shape[0]
    return pl.pallas_call(
        _conv_in_body,
        out_shape=jax.ShapeDtypeStruct((m, D_MODEL), F32),
        grid=(m // tm,),
        in_specs=[
            pl.BlockSpec((tm, D_MODEL), lambda i: (i, 0)),
            pl.BlockSpec((1, D_MODEL), lambda i: (0, 0)),
            pl.BlockSpec((D_MODEL, 2 * D_MODEL), lambda i: (0, 0)),
            pl.BlockSpec((1, 2 * D_MODEL), lambda i: (0, 0)),
        ],
        out_specs=pl.BlockSpec((tm, D_MODEL), lambda i: (i, 0)),
        compiler_params=_cparams("parallel"),
        name="conv_in",
    )(h, g, w, b)


HALO = 32
DW_ROWS = 32
DW_LANES = 512


def _ln_silu(y, ln_g, ln_b):
    mu = jnp.mean(y, axis=-1, keepdims=True)
    yc = y - mu
    var = jnp.mean(yc * yc, axis=-1, keepdims=True)
    yn = yc * lax.rsqrt(var + EPS) * ln_g + ln_b
    return yn * jax.nn.sigmoid(yn)


def _dwconv_prompt_body(ts, cur_ref, halo_ref, w_ref, bdw_ref, lng_ref, lnb_ref, z_ref, xin_ref, y_ref):
    first = pl.program_id(1) == 0

    @pl.when(first)
    def _():
        xin_ref[0:HALO, :] = jnp.zeros((HALO, D_MODEL), F32)

    @pl.when(jnp.logical_not(first))
    def _():
        xin_ref[0:HALO, :] = halo_ref[...]

    xin_ref[HALO:HALO + ts, :] = cur_ref[...]
    off = HALO - (CONV_WIDTH - 1)
    for rc in range(ts // DW_ROWS):
        for lc in range(D_MODEL // DW_LANES):
            l0, l1 = lc * DW_LANES, (lc + 1) * DW_LANES
            acc = jnp.zeros((DW_ROWS, DW_LANES), F32)
            for k in range(CONV_WIDTH):
                r0 = off + rc * DW_ROWS + k
                acc = acc + xin_ref[r0:r0 + DW_ROWS, l0:l1] * w_ref[k:k + 1, l0:l1]
            y_ref[rc * DW_ROWS:(rc + 1) * DW_ROWS, l0:l1] = acc
    y = y_ref[...] + bdw_ref[...]
    z_ref[...] = _ln_silu(y, lng_ref[...], lnb_ref[...]).astype(BF16)


def dwconv_prompt(glu, n_seq, seq, w_dw, b_dw, ln_g, ln_b, ts=256):
    nt = seq // ts
    hb = ts // HALO
    return pl.pallas_call(
        functools.partial(_dwconv_prompt_body, ts),
        out_shape=jax.ShapeDtypeStruct((n_seq * seq, D_MODEL), BF16),
        grid=(n_seq, nt),
        in_specs=[
            pl.BlockSpec((ts, D_MODEL), lambda b, i: (b * nt + i, 0)),
            pl.BlockSpec((HALO, D_MODEL), lambda b, i: (jnp.maximum((b * nt + i) * hb - 1, 0), 0)),
            pl.BlockSpec((CONV_WIDTH, D_MODEL), lambda b, i: (0, 0)),
            pl.BlockSpec((1, D_MODEL), lambda b, i: (0, 0)),
            pl.BlockSpec((1, D_MODEL), lambda b, i: (0, 0)),
            pl.BlockSpec((1, D_MODEL), lambda b, i: (0, 0)),
        ],
        out_specs=pl.BlockSpec((ts, D_MODEL), lambda b, i: (b * nt + i, 0)),
        scratch_shapes=[pltpu.VMEM((HALO + ts, D_MODEL), F32), pltpu.VMEM((ts, D_MODEL), F32)],
        compiler_params=_cparams("parallel", "parallel"),
        name="dwconv_prompt",
    )(glu, glu, w_dw, b_dw, ln_g, ln_b)


def _dwconv_sample_body(n_new, st_ref, glu_ref, w_ref, bdw_ref, lng_ref, lnb_ref, z_ref, ns_ref):
    n_st = CONV_WIDTH - 1
    rows = [st_ref[j] for j in range(n_st)] + [glu_ref[:, j, :] for j in range(n_new)]
    for t in range(n_new):
        acc = rows[t] * w_ref[0:1, :]
        for k in range(1, CONV_WIDTH):
            acc = acc + rows[t + k] * w_ref[k:k + 1, :]
        y = acc + bdw_ref[...]
        z_ref[:, t, :] = _ln_silu(y, lng_ref[...], lnb_ref[...])
    for j in range(n_st):
        ns_ref[j] = rows[j + n_new]


def dwconv_sample(state, layer, glu, w_dw, b_dw, ln_g, ln_b, bb=8):
    nb, n_new, _ = glu.shape
    n_st = CONV_WIDTH - 1
    vec = pl.BlockSpec((1, D_MODEL), lambda i: (0, 0))
    return pl.pallas_call(
        functools.partial(_dwconv_sample_body, n_new),
        out_shape=(jax.ShapeDtypeStruct((nb, n_new, D_MODEL), F32),
                   jax.ShapeDtypeStruct((n_st, nb, D_MODEL), F32)),
        grid=(nb // bb,),
        in_specs=[
            pl.BlockSpec((n_st, bb, D_MODEL), lambda i: (layer, i, 0)),
            pl.BlockSpec((bb, n_new, D_MODEL), lambda i: (i, 0, 0)),
            pl.BlockSpec((CONV_WIDTH, D_MODEL), lambda i: (0, 0)),
            vec, vec, vec,
        ],
        out_specs=(pl.BlockSpec((bb, n_new, D_MODEL), lambda i: (i, 0, 0)),
                   pl.BlockSpec((n_st, bb, D_MODEL), lambda i: (0, i, 0))),
        compiler_params=_cparams("parallel"),
        name="dwconv_sample",
    )(state, glu, w_dw, b_dw, ln_g, ln_b)


def _out_proj_body(z_ref, w_ref, b_ref, g_ref, h_ref, o_ref):
    m = _dot(z_ref[...].astype(BF16), w_ref[...]) + b_ref[...]
    o_ref[...] = h_ref[...] + _rms(m, g_ref[...])


def out_proj_res(z, w, b, g, h, tm):
    m = h.shape[0]
    vec = pl.BlockSpec((1, D_MODEL), lambda i: (0, 0))
    row = pl.BlockSpec((tm, D_MODEL), lambda i: (i, 0))
    return pl.pallas_call(
        _out_proj_body,
        out_shape=jax.ShapeDtypeStruct((m, D_MODEL), F32),
        grid=(m // tm,),
        in_specs=[row, pl.BlockSpec((D_MODEL, D_MODEL), lambda i: (0, 0)), vec, vec, row],
        out_specs=row,
        compiler_params=_cparams("parallel"),
        name="out_proj_res",
    )(z, w, b, g, h)


FF_CHUNK = 1024


def _ffn_body(h_ref, p_ref, gpre_ref, gpost_ref, win_ref, wout_ref, wpg_ref, wpp_ref, o_ref, u_ref, acc_ref):
    j = pl.program_id(1)

    @pl.when(j == 0)
    def _():
        u_ref[...] = _rms(h_ref[...], gpre_ref[...]).astype(BF16)
        acc_ref[...] = jnp.zeros_like(acc_ref)

    a = _dot(u_ref[...], win_ref[...])
    a = jnp.square(jnp.maximum(a, 0.0)).astype(BF16)
    acc_ref[...] += _dot(a, wout_ref[...])

    @pl.when(j == pl.num_programs(1) - 1)
    def _():
        h2 = h_ref[...] + _rms(acc_ref[...], gpost_ref[...])
        gate = jax.nn.sigmoid(_dot(h2.astype(BF16), wpg_ref[...]))
        proj = _dot(p_ref[...].astype(BF16), wpp_ref[...])
        o_ref[...] = h2 + gate * proj


def ffn_ple(h, p, g_pre, g_post, w_in, w_out, w_pg, w_pp, tm):
    m = h.shape[0]
    ple = p.shape[1]
    vec = pl.BlockSpec((1, D_MODEL), lambda i, j: (0, 0))
    row = pl.BlockSpec((tm, D_MODEL), lambda i, j: (i, 0))
    return pl.pallas_call(
        _ffn_body,
        out_shape=jax.ShapeDtypeStruct((m, D_MODEL), F32),
        grid=(m // tm, D_FF // FF_CHUNK),
        in_specs=[
            row,
            pl.BlockSpec((tm, ple), lambda i, j: (i, 0)),
            vec, vec,
            pl.BlockSpec((D_MODEL, FF_CHUNK), lambda i, j: (0, j)),
            pl.BlockSpec((FF_CHUNK, D_MODEL), lambda i, j: (j, 0)),
            pl.BlockSpec((D_MODEL, D_MODEL), lambda i, j: (0, 0)),
            pl.BlockSpec((ple, D_MODEL), lambda i, j: (0, 0)),
        ],
        out_specs=row,
        scratch_shapes=[pltpu.VMEM((tm, D_MODEL), BF16), pltpu.VMEM((tm, D_MODEL), F32)],
        compiler_params=_cparams("parallel", "arbitrary"),
        name="ffn_ple",
    )(h, p, g_pre, g_post, w_in, w_out, w_pg, w_pp)


NSA_W_COLS = Q_COLS + KV_COLS + GATE_PAD


def _nsa_in_body(transposed, x_ref, g_ref, w_ref, bg_ref, cos_ref, sin_ref, qq_ref, *out_refs):
    if transposed:
        kc_ref, vc_ref = out_refs[0:2]
        t_refs = out_refs[2:8]
        outs = (kc_ref, vc_ref, None, None, None, None)
    else:
        outs = out_refs[0:6]
        t_refs = None
    kvb_ref, gate_ref = out_refs[-2:]
    u = _rms(x_ref[...], g_ref[...]).astype(BF16)
    tm = u.shape[0]
    cos = jnp.concatenate([cos_ref[...], cos_ref[...]], axis=1)
    sin = jnp.concatenate([sin_ref[...], sin_ref[...]], axis=1)
    lane = lax.broadcasted_iota(jnp.int32, (tm, KV_W), 1)
    first_half = (lane & (HEAD_DIM - 1)) < (HEAD_DIM // 2)

    def rope(x):
        rot = jnp.where(first_half, pltpu.roll(x, KV_W - HEAD_DIM // 2, 1), pltpu.roll(x, HEAD_DIM // 2, 1))
        return x * cos + rot * sin

    scale = HEAD_DIM ** -0.5 * LOG2E
    for j in range(Q_COLS // KV_W):
        q = _dot(u, w_ref[:, j * KV_W:(j + 1) * KV_W]) * scale
        qq_ref[:, j * KV_W:(j + 1) * KV_W] = q.astype(BF16)
        qq_ref[:, Q_COLS + j * KV_W:Q_COLS + (j + 1) * KV_W] = rope(q).astype(BF16)
    for j in range(2 * N_BRANCH):
        x = _dot(u, w_ref[:, Q_COLS + j * KV_W:Q_COLS + (j + 1) * KV_W])
        if j in (2, 4):
            x = rope(x)
        if outs[j] is not None:
            outs[j][...] = x
        if transposed:
            t_refs[j][0] = jnp.transpose(x)
        if j >= 2:
            kvb_ref[:, (j - 2) * KV_W:(j - 1) * KV_W] = x.astype(BF16)
    gl = _dot(u, w_ref[:, Q_COLS + KV_COLS:]) + bg_ref[...]
    gate_ref[...] = jax.nn.sigmoid(gl)


def nsa_in(h, g, w, bg, cos_t, sin_t, pos_map, tm, seq=None):
    m = h.shape[0]
    kv_spec = pl.BlockSpec((tm, KV_W), lambda i: (i, 0))
    kv_shape = jax.ShapeDtypeStruct((m, KV_W), F32)
    if seq is None:
        kv_shapes = (kv_shape,) * 6
        kv_specs = (kv_spec,) * 6
    else:
        nt = seq // tm
        t_shape = jax.ShapeDtypeStruct((m // seq, KV_W, seq), F32)
        t_spec = pl.BlockSpec((1, KV_W, tm), lambda i: (i // nt, 0, i % nt))
        kv_shapes = (kv_shape,) * 2 + (t_shape,) * 6
        kv_specs = (kv_spec,) * 2 + (t_spec,) * 6
    return pl.pallas_call(
        functools.partial(_nsa_in_body, seq is not None),
        out_shape=(jax.ShapeDtypeStruct((m, 2 * Q_COLS), BF16),) + kv_shapes + (
                   jax.ShapeDtypeStruct((m, 4 * KV_W), BF16),
                   jax.ShapeDtypeStruct((m, GATE_PAD), F32)),
        grid=(m // tm,),
        in_specs=[
            pl.BlockSpec((tm, D_MODEL), lambda i: (i, 0)),
            pl.BlockSpec((1, D_MODEL), lambda i: (0, 0)),
            pl.BlockSpec((D_MODEL, NSA_W_COLS), lambda i: (0, 0)),
            pl.BlockSpec((1, GATE_PAD), lambda i: (0, 0)),
            pl.BlockSpec((tm, 2 * HEAD_DIM), lambda i: (pos_map(i), 0)),
            pl.BlockSpec((tm, 2 * HEAD_DIM), lambda i: (pos_map(i), 0)),
        ],
        out_specs=(pl.BlockSpec((tm, 2 * Q_COLS), lambda i: (i, 0)),) + kv_specs + (
                   pl.BlockSpec((tm, 4 * KV_W), lambda i: (i, 0)),
                   pl.BlockSpec((tm, GATE_PAD), lambda i: (i, 0))),
        compiler_params=_cparams("parallel"),
        name="nsa_in",
    )(h, g, w, bg, cos_t, sin_t)


CHUNK_W = CMP_STRIDE * HEAD_DIM
TOK_PAIRS = CMP_STRIDE // 2


def _compress_one(load_t, nc, n_out, w1_ref, pe_ref, w2_ref, out_ref, c_ref, ab_ref):
    lane = lax.broadcasted_iota(jnp.int32, (nc, 2 * HEAD_DIM), 1)
    low = lane < HEAD_DIM
    for tp in range(TOK_PAIRS):
        for half in range(KV_GROUPS // 2):
            a = load_t(2 * tp, half)
            b = load_t(2 * tp + 1, half)
            even = jnp.where(low, a, pltpu.roll(b, HEAD_DIM, 1))
            odd = jnp.where(low, pltpu.roll(a, HEAD_DIM, 1), b)
            g0 = 2 * half
            c_ref[g0 * nc:(g0 + 1) * nc, tp * 128:(tp + 1) * 128] = even.astype(BF16)
            c_ref[(g0 + 1) * nc:(g0 + 2) * nc, tp * 128:(tp + 1) * 128] = odd.astype(BF16)
    ab_ref[0:KV_GROUPS * nc, :] = _dot(c_ref[...], w1_ref[...])
    ab_ref[KV_GROUPS * nc:KV_GROUPS * nc + 8, :] = jnp.zeros((8, 2 * CMP_HIDDEN), F32)
    pe2 = _dot(pe_ref[...].astype(BF16), w1_ref[...])
    pe_c = pe2[0:1, 0:CMP_HIDDEN] + pe2[1:2, CMP_HIDDEN:2 * CMP_HIDDEN]
    hs = []
    for g in range(KV_GROUPS):
        top = ab_ref[g * nc:g * nc + n_out, 0:CMP_HIDDEN]
        bot = ab_ref[g * nc + 1:g * nc + 1 + n_out, CMP_HIDDEN:2 * CMP_HIDDEN]
        hs.append(jax.nn.gelu(top + bot + pe_c).astype(BF16))
    hcat = jnp.concatenate(hs, axis=1)
    out_ref[0] = _dot(hcat, w2_ref[...]).astype(BF16)


def _compress_prompt_body(nc, k0_ref, k1_ref, v0_ref, v1_ref, wk1_ref, pek_ref, wk2_ref, wv1_ref, pev_ref, wv2_ref,
                          ck_ref, cv_ref, c_ref, ab_ref):
    def loader(halves):
        return lambda t, half: halves[half][pl.ds(t, nc, stride=CMP_STRIDE), :]

    _compress_one(loader((k0_ref, k1_ref)), nc, nc, wk1_ref, pek_ref, wk2_ref, ck_ref, c_ref, ab_ref)
    _compress_one(loader((v0_ref, v1_ref)), nc, nc, wv1_ref, pev_ref, wv2_ref, cv_ref, c_ref, ab_ref)


def _compress_weight_specs(idx):
    return [pl.BlockSpec((CHUNK_W, 2 * CMP_HIDDEN), idx),
            pl.BlockSpec((8, CHUNK_W), idx),
            pl.BlockSpec((KV_GROUPS * CMP_HIDDEN, KV_W), idx)]


def compress_prompt(kc, vc, n_seq, seq, wk, wv):
    nc = seq // CMP_STRIDE
    idx = lambda b: (0, 0)
    out = jax.ShapeDtypeStruct((n_seq, nc, KV_W), BF16)
    halves = [pl.BlockSpec((seq, 128), lambda b: (b, 0)), pl.BlockSpec((seq, 128), lambda b: (b, 1))]
    return pl.pallas_call(
        functools.partial(_compress_prompt_body, nc),
        out_shape=(out, out),
        grid=(n_seq,),
        in_specs=halves + halves + _compress_weight_specs(idx) + _compress_weight_specs(idx),
        out_specs=(pl.BlockSpec((1, nc, KV_W), lambda b: (b, 0, 0)), pl.BlockSpec((1, nc, KV_W), lambda b: (b, 0, 0))),
        scratch_shapes=[pltpu.VMEM((KV_GROUPS * nc, CHUNK_W), BF16),
                        pltpu.VMEM((KV_GROUPS * nc + 8, 2 * CMP_HIDDEN), F32)],
        compiler_params=_cparams("parallel"),
        name="compress_prompt",
    )(kc, kc, vc, vc, *wk, *wv)


def _compress_sample_body(n_pages, page, nc, n_out, pt_ref, *refs):
    k_pages = refs[0:n_pages]
    v_pages = refs[n_pages:2 * n_pages]
    k_new, v_new = refs[2 * n_pages:2 * n_pages + 2]
    wk1_ref, pek_ref, wk2_ref, wv1_ref, pev_ref, wv2_ref = refs[2 * n_pages + 2:2 * n_pages + 8]
    ck_ref, cv_ref, c_ref, ab_ref, x0_ref, x1_ref = refs[2 * n_pages + 8:]
    past = n_pages * page
    n_rows = nc * CMP_STRIDE
    halves = (x0_ref, x1_ref)

    def run(pages, new_ref, w1_ref, pe_ref, w2_ref, out_ref):
        for p, page_ref in enumerate(pages):
            for half in range(2):
                halves[half][p * page:(p + 1) * page, :] = jnp.transpose(page_ref[0, half * 128:(half + 1) * 128, :])
        for half in range(2):
            halves[half][past:past + CMP_STRIDE, :] = new_ref[0, :, half * 128:(half + 1) * 128]
            halves[half][past + CMP_STRIDE:n_rows, :] = jnp.zeros((n_rows - past - CMP_STRIDE, 128), F32)
        load_t = lambda t, half: halves[half][pl.ds(t, nc, stride=CMP_STRIDE), :]
        _compress_one(load_t, nc, n_out, w1_ref, pe_ref, w2_ref, out_ref, c_ref, ab_ref)

    run(k_pages, k_new, wk1_ref, pek_ref, wk2_ref, ck_ref)
    run(v_pages, v_new, wv1_ref, pev_ref, wv2_ref, cv_ref)


def compress_sample(pool_k, pool_v, page0, page_table, k_new, v_new, wk, wv):
    nb, n_pages = page_table.shape
    page = pool_k.shape[2]
    assert page == 128
    n_chunk = n_pages * (page // CMP_STRIDE) + 1
    n_out = n_chunk - 1
    nc = -(-(n_chunk + 1) // 16) * 16
    idx = lambda b, pt: (0, 0)
    page_specs = [pl.BlockSpec((1, KV_W, page), functools.partial(lambda p, b, pt: (page0 + pt[b * n_pages + p], 0, 0), p))
                  for p in range(n_pages)]
    new_spec = pl.BlockSpec((1, CMP_STRIDE, KV_W), lambda b, pt: (b, 0, 0))
    out = jax.ShapeDtypeStruct((nb, n_out, KV_W), BF16)
    grid_spec = pltpu.PrefetchScalarGridSpec(
        num_scalar_prefetch=1,
        grid=(nb,),
        in_specs=page_specs * 2 + [new_spec, new_spec]
        + _compress_weight_specs(idx) + _compress_weight_specs(idx),
        out_specs=(pl.BlockSpec((1, n_out, KV_W), lambda b, pt: (b, 0, 0)),
                   pl.BlockSpec((1, n_out, KV_W), lambda b, pt: (b, 0, 0))),
        scratch_shapes=[pltpu.VMEM((KV_GROUPS * nc, CHUNK_W), BF16),
                        pltpu.VMEM((KV_GROUPS * nc + 8, 2 * CMP_HIDDEN), F32),
                        pltpu.VMEM((nc * CMP_STRIDE, 128), F32),
                        pltpu.VMEM((nc * CMP_STRIDE, 128), F32)],
    )
    return pl.pallas_call(
        functools.partial(_compress_sample_body, n_pages, page, nc, n_out),
        out_shape=(out, out),
        grid_spec=grid_spec,
        compiler_params=_cparams("parallel"),
        name="compress_sample",
    )(page_table.reshape(-1), *([pool_k] * n_pages), *([pool_v] * n_pages), k_new, v_new, *wk, *wv)


def _group_masks():
    lane = lax.broadcasted_iota(jnp.int32, (1, KV_W), 1)
    return [(lane // HEAD_DIM) == g for g in range(KV_GROUPS)]


def _stack_q(q, gmask):
    if q.shape[0] % 16:
        q = q.astype(F32)
    zero = jnp.zeros((q.shape[0], KV_W), q.dtype)
    return jnp.concatenate([jnp.where(gmask[g], q[:, r * KV_W:(r + 1) * KV_W], zero)
                            for g in range(KV_GROUPS) for r in range(GROUP_SIZE)], axis=0).astype(BF16)


def _unstack_o(o, tq, gmask):
    cols = []
    for r in range(GROUP_SIZE):
        acc = jnp.zeros((tq, KV_W), F32)
        for g in range(KV_GROUPS):
            row = (g * GROUP_SIZE + r) * tq
            acc = acc + jnp.where(gmask[g], o[row:row + tq, :], 0.0)
        cols.append(acc)
    return jnp.concatenate(cols, axis=1)


def _rep_rows(x, tq):
    return jnp.concatenate([x[g * tq:(g + 1) * tq] for g in range(KV_GROUPS) for _ in range(GROUP_SIZE)], axis=0)


def _softmax_rows(s, mask):
    s = jnp.where(mask, s, NEG)
    m = jnp.max(s, axis=1, keepdims=True)
    e = jnp.where(mask, jnp.exp2(s - m), 0.0)
    d = jnp.sum(e, axis=1, keepdims=True)
    return e * (1.0 / jnp.where(d > 0, d, 1.0))


def _cmp_branch(qc, ck, cv, qpos0, tq):
    rows, n = qc.shape[0], ck.shape[0]
    s = _dot_t(qc, ck)
    col = lax.broadcasted_iota(jnp.int32, (rows, n), 1)
    qp = qpos0 + (lax.broadcasted_iota(jnp.int32, (rows, n), 0) & (tq - 1))
    p = _softmax_rows(s, col * CMP_STRIDE + (CMP_BLOCK - 1) <= qp)
    o = _dot(p.astype(BF16), cv)
    psum = []
    for g in range(KV_GROUPS):
        acc = p[g * GROUP_SIZE * tq:(g * GROUP_SIZE + 1) * tq]
        for r in range(1, GROUP_SIZE):
            acc = acc + p[(g * GROUP_SIZE + r) * tq:(g * GROUP_SIZE + r + 1) * tq]
        psum.append(acc)
    return o, jnp.concatenate(psum, axis=0)


def _select_bias(psum, cov_t, qpos0, tq, n_live, score_ref):
    hi, lo = _split_bf16(psum)
    imp = _dot_t(cov_t, hi) + _dot_t(cov_t, lo)
    shape = imp.shape
    blk = lax.broadcasted_iota(jnp.int32, shape, 0)
    cur = (qpos0 + (lax.broadcasted_iota(jnp.int32, shape, 1) & (tq - 1))) // SEL_BLOCK
    forced = (blk == 0) | (blk == cur) | (blk == cur - 1)
    score = jnp.where(blk > cur, -jnp.inf, jnp.where(forced, jnp.inf, imp))
    score_ref[...] = score

    def count(k, rank):
        sk = jnp.broadcast_to(score_ref[pl.ds(k, 1), :], shape)
        ge = jnp.where(sk >= score, 1.0, 0.0)
        gt = jnp.where(sk > score, 1.0, 0.0)
        return rank + jnp.where(blk > k, ge, gt)

    rank = lax.fori_loop(0, n_live, count, jnp.zeros(shape, F32))
    bias_t = jnp.where(rank < N_SELECT, 0.0, NEG)
    bias_t = jnp.concatenate([bias_t, jnp.zeros(shape, F32)], axis=0)
    return jnp.transpose(bias_t).astype(BF16)


def _gate_mix(gate, gexp_ref, outs):
    hi, lo = _split_bf16(gate)
    total = None
    for br, o in enumerate(outs):
        ge = _dot(hi, gexp_ref[br]) + _dot(lo, gexp_ref[br])
        total = ge * o if total is None else total + ge * o
    return total


TQ_P = 64
KEY_CHUNK = 512


def _attn_prompt_body(seq, qq_ref, gate_ref, ck_ref, cv_ref, ks_ref, vs_ref, kw_ref, vw_ref,
                      cov_ref, et_ref, gexp_ref, o_ref, score_ref):
    tq = TQ_P
    rows = KV_GROUPS * GROUP_SIZE * tq
    n = pl.program_id(1)
    q0 = n * tq
    gmask = _group_masks()
    qc = _stack_q(qq_ref[:, 0:Q_COLS], gmask)
    qr = _stack_q(qq_ref[:, Q_COLS:2 * Q_COLS], gmask)

    o_c, psum = _cmp_branch(qc, ck_ref[0], cv_ref[0], q0, tq)
    bias = _select_bias(psum, cov_ref[...], q0, tq, (q0 + tq - 1) // SEL_BLOCK + 1, score_ref)

    q_aug = jnp.concatenate([qr, _rep_rows(bias, tq)], axis=1)
    ones = jnp.ones((KEY_CHUNK, 128), BF16)

    def chunk(c, carry, causal):
        m, acc = carry
        k0 = pl.multiple_of(c * KEY_CHUNK, KEY_CHUNK)
        k_aug = jnp.concatenate([ks_ref[pl.ds(k0, KEY_CHUNK), :], et_ref[pl.ds(k0, KEY_CHUNK), :]], axis=1)
        v_aug = jnp.concatenate([vs_ref[pl.ds(k0, KEY_CHUNK), :], ones], axis=1)
        s = _dot_t(q_aug, k_aug)
        if causal:
            row_q = q0 + (lax.broadcasted_iota(jnp.int32, (rows, KEY_CHUNK), 0) & (tq - 1))
            col_k = k0 + lax.broadcasted_iota(jnp.int32, (rows, KEY_CHUNK), 1)
            s = jnp.where(col_k <= row_q, s, NEG)
        m_new = jnp.maximum(m, jnp.max(s, axis=1, keepdims=True))
        alpha = jnp.exp2(m - m_new)
        p = jnp.exp2(s - m_new).astype(BF16)
        return m_new, alpha * acc + _dot(p, v_aug)

    diag = q0 // KEY_CHUNK
    init = (jnp.full((rows, 1), NEG, F32), jnp.zeros((rows, KV_W + 128), F32))
    carry = lax.fori_loop(0, diag, functools.partial(chunk, causal=False), init)
    _, acc = chunk(diag, carry, True)
    l = acc[:, KV_W:KV_W + 128]
    o_s = acc[:, 0:KV_W] * (1.0 / jnp.concatenate([l, l], axis=1))

    nk = WINDOW + tq
    base = pl.multiple_of(jnp.maximum(q0 - WINDOW, 0), tq)
    s = _dot_t(qr, kw_ref[pl.ds(base, nk), :])
    kpos = base + lax.broadcasted_iota(jnp.int32, (rows, nk), 1)
    qp = q0 + (lax.broadcasted_iota(jnp.int32, (rows, nk), 0) & (tq - 1))
    p = _softmax_rows(s, (kpos <= qp) & (kpos > qp - WINDOW))
    o_w = _dot(p.astype(BF16), vw_ref[pl.ds(base, nk), :])

    outs = [_unstack_o(o, tq, gmask) for o in (o_c, o_s, o_w)]
    o_ref[...] = _gate_mix(gate_ref[...], gexp_ref, outs).astype(BF16)


def attn_prompt(qq, gate, ck, cv, kvb, n_seq, seq, cov_t, e_t, gexp):
    tq = TQ_P
    nt = seq // tq
    n_cmp = ck.shape[1]

    def kv_spec(col):
        return pl.BlockSpec((seq, KV_W), lambda b, i: (b, col))

    return pl.pallas_call(
        functools.partial(_attn_prompt_body, seq),
        out_shape=jax.ShapeDtypeStruct((n_seq * seq, Q_COLS), BF16),
        grid=(n_seq, nt),
        in_specs=[
            pl.BlockSpec((tq, 2 * Q_COLS), lambda b, i: (b * nt + i, 0)),
            pl.BlockSpec((tq, GATE_PAD), lambda b, i: (b * nt + i, 0)),
            pl.BlockSpec((1, n_cmp, KV_W), lambda b, i: (b, 0, 0)),
            pl.BlockSpec((1, n_cmp, KV_W), lambda b, i: (b, 0, 0)),
            kv_spec(0), kv_spec(1), kv_spec(2), kv_spec(3),
            pl.BlockSpec(cov_t.shape, lambda b, i: (0, 0)),
            pl.BlockSpec(e_t.shape, lambda b, i: (0, 0)),
            pl.BlockSpec(gexp.shape, lambda b, i: (0, 0, 0)),
        ],
        out_specs=pl.BlockSpec((tq, Q_COLS), lambda b, i: (b * nt + i, 0)),
        scratch_shapes=[pltpu.VMEM((MAX_SEL_BLOCKS, KV_GROUPS * tq), F32)],
        compiler_params=_cparams("parallel", "parallel"),
        name="attn_prompt",
    )(qq, gate, ck, cv, kvb, kvb, kvb, kvb, cov_t, e_t, gexp)


TQ_S = 8
NEW_PAD = 128


def _attn_sample_body(n_pages, page, n_new, pt_ref, *refs):
    ks_pages = refs[0:n_pages]
    vs_pages = refs[n_pages:2 * n_pages]
    (qq_ref, gate_ref, ck_ref, cv_ref, kvn_ref, kwn_ref, vwn_ref, kwb_ref, vwb_ref,
     cov_ref, et_ref, gexp_ref, eye_ref, o_ref, kwo_ref, vwo_ref, score_ref) = refs[2 * n_pages:]
    tq = TQ_S
    rows = KV_GROUPS * GROUP_SIZE * tq
    past = n_pages * page
    gmask = _group_masks()
    qc = _stack_q(qq_ref[0, :, 0:Q_COLS], gmask)
    qr = _stack_q(qq_ref[0, :, Q_COLS:2 * Q_COLS], gmask)

    o_c, psum = _cmp_branch(qc, ck_ref[0], cv_ref[0], past, tq)
    n_live = (past + tq - 1) // SEL_BLOCK + 1
    bias = _select_bias(psum, cov_ref[...], past, tq, n_live, score_ref)

    pad = jnp.zeros((NEW_PAD - tq, KV_W), BF16)
    eye = eye_ref[...]

    def new_cols(col):
        x = jnp.concatenate([kvn_ref[0, :, col * KV_W:(col + 1) * KV_W], pad], axis=0)
        return _dot_t(eye, x).astype(BF16)

    k_all = jnp.concatenate([p[0].astype(BF16) for p in ks_pages] + [new_cols(0)], axis=1)
    v_all = jnp.concatenate([p[0].astype(BF16) for p in vs_pages] + [new_cols(1)], axis=1)
    nk = past + NEW_PAD
    s = _dot(qr, k_all) + _rep_rows(_dot_t(bias, et_ref[...]), tq)
    kpos = lax.broadcasted_iota(jnp.int32, (rows, nk), 1)
    qp = past + (lax.broadcasted_iota(jnp.int32, (rows, nk), 0) & (tq - 1))
    o_s = _dot_t(_softmax_rows(s, kpos <= qp).astype(BF16), v_all)

    wb = kwb_ref.shape[2]
    kw_all = jnp.concatenate([kwb_ref[0].astype(BF16), new_cols(2)], axis=1)
    vw_all = jnp.concatenate([vwb_ref[0].astype(BF16), new_cols(3)], axis=1)
    nw = wb + NEW_PAD
    s = _dot(qr, kw_all)
    kpos = (past - wb) + lax.broadcasted_iota(jnp.int32, (rows, nw), 1)
    qp = past + (lax.broadcasted_iota(jnp.int32, (rows, nw), 0) & (tq - 1))
    mask = (kpos >= 0) & (kpos <= qp) & (kpos > qp - WINDOW)
    o_w = _dot_t(_softmax_rows(s, mask).astype(BF16), vw_all)

    outs = [_unstack_o(o, tq, gmask) for o in (o_c, o_s, o_w)]
    o_ref[0] = _gate_mix(gate_ref[0], gexp_ref, outs).astype(BF16)

    last = lax.broadcasted_iota(jnp.int32, (KV_W, 128), 1) >= 128 - n_new
    fill = jnp.zeros((128 - tq, KV_W), F32)
    for new_ref, buf_ref, out_ref in ((kwn_ref, kwb_ref, kwo_ref), (vwn_ref, vwb_ref, vwo_ref)):
        tail = jnp.concatenate([fill, pltpu.roll(new_ref[0], tq - n_new, 0)], axis=0)
        hi, lo = _split_bf16(tail)
        tail_t = _dot_t(eye, hi) + _dot_t(eye, lo)
        shifted = pltpu.roll(buf_ref[0], wb - n_new, 1)
        out_ref[0, :, 0:wb - 128] = shifted[:, 0:wb - 128]
        out_ref[0, :, wb - 128:wb] = jnp.where(last, tail_t, shifted[:, wb - 128:wb])


def attn_sample(pool_ks, pool_vs, page0, page_table, qq, gate, ck, cv, kvn, kwn, vwn, kwb, vwb, win0, n_new,
                cov_t, e_t, gexp, eye):
    nb, n_pages = page_table.shape
    page = pool_ks.shape[2]
    wb = kwb.shape[2]
    n_cmp = ck.shape[1]
    assert page == 128 and wb % 128 == 0

    def page_spec(p):
        return pl.BlockSpec((1, KV_W, page), lambda b, pt: (page0 + pt[b * n_pages + p], 0, 0))

    def bspec(shape):
        return pl.BlockSpec((1,) + shape, lambda b, pt: (b, 0, 0))

    def cspec(shape):
        return pl.BlockSpec(shape, lambda b, pt: (0,) * len(shape))

    win_spec = pl.BlockSpec((1, KV_W, wb), lambda b, pt: (win0 + b, 0, 0))
    grid_spec = pltpu.PrefetchScalarGridSpec(
        num_scalar_prefetch=1,
        grid=(nb,),
        in_specs=[page_spec(p) for p in range(n_pages)] * 2 + [
            bspec((TQ_S, 2 * Q_COLS)), bspec((TQ_S, GATE_PAD)), bspec((n_cmp, KV_W)), bspec((n_cmp, KV_W)),
            bspec((TQ_S, 4 * KV_W)), bspec((TQ_S, KV_W)), bspec((TQ_S, KV_W)),
            win_spec, win_spec,
            cspec(cov_t.shape), cspec(e_t.shape), cspec(gexp.shape), cspec(eye.shape),
        ],
        out_specs=(bspec((TQ_S, Q_COLS)), bspec((KV_W, wb)), bspec((KV_W, wb))),
        scratch_shapes=[pltpu.VMEM((MAX_SEL_BLOCKS, KV_GROUPS * TQ_S), F32)],
    )
    return pl.pallas_call(
        functools.partial(_attn_sample_body, n_pages, page, n_new),
        out_shape=(jax.ShapeDtypeStruct((nb, TQ_S, Q_COLS), BF16),
                   jax.ShapeDtypeStruct((nb, KV_W, wb), F32),
                   jax.ShapeDtypeStruct((nb, KV_W, wb), F32)),
        grid_spec=grid_spec,
        compiler_params=_cparams("parallel"),
        name="attn_sample",
    )(page_table.reshape(-1), *([pool_ks] * n_pages), *([pool_vs] * n_pages),
      qq, gate, ck, cv, kvn, kwn, vwn, kwb, vwb, cov_t, e_t, gexp, eye)


def _rope_tables(pos):
    half = HEAD_DIM // 2
    inv = ROPE_THETA ** (-jnp.arange(half, dtype=F32) / half)
    ang = pos.astype(F32)[:, None] * inv[None, :]
    c, s = jnp.cos(ang), jnp.sin(ang)
    return jnp.tile(jnp.concatenate([c, c], axis=1), (1, 2)), jnp.tile(jnp.concatenate([-s, s], axis=1), (1, 2))


def _head_perm():
    c = np.arange(Q_COLS)
    r, g, d = c // KV_W, (c % KV_W) // HEAD_DIM, c % HEAD_DIM
    return (g * GROUP_SIZE + r) * HEAD_DIM + d


def _cover_t(n_cmp, n_cmp_pad):
    cs = np.arange(n_cmp_pad) * CMP_STRIDE
    ss = np.arange(MAX_SEL_BLOCKS) * SEL_BLOCK
    m = (cs[None, :] < ss[:, None] + SEL_BLOCK) & (cs[None, :] + CMP_BLOCK > ss[:, None])
    m = m & (np.arange(n_cmp_pad)[None, :] < n_cmp)
    return jnp.asarray(m, dtype=BF16)


def _block_onehot_t(n_keys):
    k = np.arange(n_keys)
    return jnp.asarray((k[:, None] // SEL_BLOCK) == np.arange(128)[None, :], dtype=BF16)


def _gate_expand():
    ge = np.zeros((N_BRANCH, GATE_PAD, Q_COLS), np.float32)
    c = np.arange(Q_COLS)
    r, g = c // KV_W, (c % KV_W) // HEAD_DIM
    for br in range(N_BRANCH):
        ge[br, (g * GROUP_SIZE + r) * N_BRANCH + br, c] = 1.0
    return jnp.asarray(ge, dtype=BF16)


def _compress_weights(pe, w1, w2):
    half = CMP_BLOCK // 2 * HEAD_DIM
    w1cat = jnp.concatenate([w1[:half], w1[half:]], axis=1).astype(BF16)
    pe2 = jnp.pad(pe.reshape(2, half), ((0, 6), (0, 0)))
    w2bd = jnp.kron(jnp.eye(KV_GROUPS, dtype=F32), w2).astype(BF16)
    return w1cat, pe2, w2bd


def _row(v):
    return v.reshape(1, -1)


TM_PROMPT = 512
TM_SAMPLE = 512


def kernel(x_prompt, x_sample, state_conv, cache_k_cmp, cache_v_cmp, cache_k_sel, cache_v_sel, state_k_win, state_v_win, page_table, p_prompt, p_sample, norm_mix_pre, norm_mix_post, norm_ffn_pre, norm_ffn_post, conv_w_in, conv_b_in, conv_w_dw, conv_b_dw, conv_ln_g, conv_ln_b, conv_w_out, conv_b_out, nsa_w_in, nsa_b_gate, nsa_pe_k, nsa_pe_v, nsa_wk_c1, nsa_wk_c2, nsa_wv_c1, nsa_wv_c2, nsa_w_o, ffn_w_in, ffn_w_out, ple_w_gate, ple_w_proj):
    n_p, seq, _ = x_prompt.shape
    n_s, n_new, _ = x_sample.shape
    n_pages = page_table.shape[1]
    page = cache_k_cmp.shape[2]
    past = n_pages * page
    tp, ts = n_p * seq, n_s * n_new
    assert seq % KEY_CHUNK == 0 and seq // SEL_BLOCK <= MAX_SEL_BLOCKS and seq >= WINDOW + TQ_P
    assert (past + n_new + SEL_BLOCK - 1) // SEL_BLOCK <= MAX_SEL_BLOCKS and n_new <= TQ_S
    assert past % SEL_BLOCK == 0 and page % CMP_STRIDE == 0 and state_k_win.shape[2] == WINDOW

    hp = x_prompt.reshape(tp, D_MODEL)
    hs = x_sample.reshape(ts, D_MODEL)
    pp = p_prompt.reshape(DEPTH, tp, -1)
    ps = p_sample.reshape(DEPTH, ts, -1)

    perm = _head_perm()
    gexp = _gate_expand()
    cos_p, sin_p = _rope_tables(jnp.arange(seq))
    cos_s, sin_s = _rope_tables(past + jnp.arange(ts) % n_new)
    n_cmp_p = seq // CMP_STRIDE - 1
    cov_p = _cover_t(n_cmp_p, seq // CMP_STRIDE)
    n_cmp_s = past // CMP_STRIDE
    cov_s = _cover_t(n_cmp_s, n_cmp_s)
    et_p = _block_onehot_t(seq)
    et_s = _block_onehot_t(past + NEW_PAD)
    eye = jnp.eye(KV_W, dtype=BF16)
    tiles_per_seq = seq // TM_PROMPT

    def pad_rows(x, n):
        return jnp.pad(x, ((0, 0), (0, n - x.shape[1]), (0, 0)))

    def token_minor(c):
        return jnp.transpose(c, (0, 1, 3, 4, 2)).reshape(c.shape[0] * c.shape[1], KV_W, c.shape[2])

    def token_major(t, lead):
        t = t.reshape(lead + (KV_GROUPS, HEAD_DIM, t.shape[-1]))
        return jnp.moveaxis(t, -1, len(lead))

    pools = [token_minor(c) for c in (cache_k_cmp, cache_v_cmp, cache_k_sel, cache_v_sel)]
    n_pool = cache_k_cmp.shape[1]
    kwin_t, vwin_t = token_minor(state_k_win), token_minor(state_v_win)
    conv_state_t = jnp.transpose(state_conv, (0, 2, 1, 3)).reshape(-1, n_s, D_MODEL)

    conv_p, conv_s, nsa_p, nsa_s = [], [], [], []
    for i in range(DEPTH):
        j = i // N_MIXERS
        g_pre, g_post = _row(norm_mix_pre[i]), _row(norm_mix_post[i])
        if i % N_MIXERS == 0:
            w_in = conv_w_in[j].astype(BF16)
            b_in = _row(conv_b_in[j])
            w_out = conv_w_out[j].astype(BF16)
            dw = (conv_w_dw[j], _row(conv_b_dw[j]), _row(conv_ln_g[j]), _row(conv_ln_b[j]))
            glu_p = conv_in(hp, g_pre, w_in, b_in, TM_PROMPT)
            glu_s = conv_in(hs, g_pre, w_in, b_in, TM_SAMPLE)
            z_p = dwconv_prompt(glu_p, n_p, seq, *dw)
            z_s, st_s = dwconv_sample(conv_state_t, j, glu_s.reshape(n_s, n_new, D_MODEL), *dw)
            conv_p.append(glu_p.reshape(n_p, seq, D_MODEL)[:, seq - (CONV_WIDTH - 1):])
            conv_s.append(st_s)
            b_out = _row(conv_b_out[j])
            hp = out_proj_res(z_p, w_out, b_out, g_post, hp, TM_PROMPT)
            hs = out_proj_res(z_s.reshape(ts, D_MODEL), w_out, b_out, g_post, hs, TM_SAMPLE)
        else:
            w = nsa_w_in[j]
            w_in = jnp.concatenate([w[:, :Q_COLS][:, perm], w[:, Q_COLS:],
                                    jnp.zeros((D_MODEL, GATE_PAD - GATE_COLS), F32)], axis=1).astype(BF16)
            bg = jnp.pad(nsa_b_gate[j], (0, GATE_PAD - GATE_COLS)).reshape(1, GATE_PAD)
            w_o = nsa_w_o[j][perm].astype(BF16)
            wk = _compress_weights(nsa_pe_k[j], nsa_wk_c1[j], nsa_wk_c2[j])
            wv = _compress_weights(nsa_pe_v[j], nsa_wv_c1[j], nsa_wv_c2[j])
            zero_b = jnp.zeros((1, D_MODEL), F32)

            qq, kc, vc, kc_t, vc_t, ks_t, vs_t, kw_t, vw_t, kvb, gate = nsa_in(
                hp, g_pre, w_in, bg, cos_p, sin_p, lambda t: t % tiles_per_seq, TM_PROMPT, seq=seq)
            ck, cv = compress_prompt(kc, vc, n_p, seq, wk, wv)
            o_p = attn_prompt(qq, gate, ck, cv, kvb, n_p, seq, cov_p, et_p, gexp)
            hp = out_proj_res(o_p, w_o, zero_b, g_post, hp, TM_PROMPT)
            keep = min(WINDOW, seq)
            nsa_p.append((kc_t, vc_t, ks_t, vs_t, kw_t[:, :, seq - keep:], vw_t[:, :, seq - keep:]))

            qq, kc, vc, ks, vs, kw, vw, kvb, gate = nsa_in(
                hs, g_pre, w_in, bg, cos_s, sin_s, lambda t: t, TM_SAMPLE)
            r3 = lambda x: x.reshape(n_s, n_new, x.shape[-1])
            ck, cv = compress_sample(pools[0], pools[1], j * n_pool, page_table,
                                     pad_rows(r3(kc), CMP_STRIDE), pad_rows(r3(vc), CMP_STRIDE), wk, wv)
            o_s, kwo, vwo = attn_sample(
                pools[2], pools[3], j * n_pool, page_table,
                pad_rows(r3(qq), TQ_S), pad_rows(r3(gate), TQ_S), ck, cv,
                pad_rows(r3(kvb), TQ_S), pad_rows(r3(kw), TQ_S), pad_rows(r3(vw), TQ_S),
                kwin_t, vwin_t, j * n_s, n_new, cov_s, et_s, gexp, eye)
            hs = out_proj_res(o_s[:, :n_new].reshape(ts, Q_COLS), w_o, zero_b, g_post, hs, TM_SAMPLE)
            kvshape = (n_s, n_new, KV_GROUPS, HEAD_DIM)
            nsa_s.append((kc.reshape(kvshape), vc.reshape(kvshape), ks.reshape(kvshape), vs.reshape(kvshape),
                          kwo, vwo))

        fw = (_row(norm_ffn_pre[i]), _row(norm_ffn_post[i]), ffn_w_in[i].astype(BF16), ffn_w_out[i].astype(BF16),
              ple_w_gate[i].astype(BF16), ple_w_proj[i].astype(BF16))
        hp = ffn_ple(hp, pp[i], *fw, TM_PROMPT)
        hs = ffn_ple(hs, ps[i], *fw, TM_SAMPLE)

    stack = lambda lst, k: jnp.stack([st[k] for st in lst])
    stack_t = lambda lst, k, nb: token_major(stack(lst, k), (len(lst), nb))
    conv_state_s = jnp.transpose(jnp.stack(conv_s), (0, 2, 1, 3))
    return (hp.reshape(x_prompt.shape), hs.reshape(x_sample.shape), jnp.stack(conv_p),
            stack_t(nsa_p, 0, n_p), stack_t(nsa_p, 1, n_p), stack_t(nsa_p, 2, n_p), stack_t(nsa_p, 3, n_p),
            stack_t(nsa_p, 4, n_p), stack_t(nsa_p, 5, n_p),
            conv_state_s,
            stack(nsa_s, 0), stack(nsa_s, 1), stack(nsa_s, 2), stack(nsa_s, 3),
            stack_t(nsa_s, 4, n_s), stack_t(nsa_s, 5, n_s))
```

```python
import functools

import numpy as np
import jax
import jax.numpy as jnp
from jax import lax
from jax.experimental import pallas as pl
from jax.experimental.pallas import tpu as pltpu

F32 = jnp.float32
BF16 = jnp.bfloat16

D_MODEL = 1024
DEPTH = 4
N_MIXERS = 2
CONV_WIDTH = 31
N_HEADS = 16
HEAD_DIM = 64
KV_GROUPS = 4
GROUP_SIZE = 4
N_BRANCH = 3
CMP_BLOCK = 32
CMP_STRIDE = 16
CMP_HIDDEN = 2 * HEAD_DIM
SEL_BLOCK = 64
N_SELECT = 16
WINDOW = 512
D_FF = 4 * D_MODEL
ROPE_THETA = 10000.0
EPS = 1e-6
KV_W = KV_GROUPS * HEAD_DIM
Q_COLS = N_HEADS * HEAD_DIM
KV_COLS = 2 * N_BRANCH * KV_W
GATE_COLS = N_BRANCH * N_HEADS
GATE_PAD = 128
NEG = -1e30
LOG2E = 1.4426950408889634
MAX_SEL_BLOCKS = 64
VMEM_LIMIT = 56 * 1024 * 1024


def _cparams(*sem):
    return pltpu.CompilerParams(dimension_semantics=sem, vmem_limit_bytes=VMEM_LIMIT)


def _rms(x, g):
    return x * lax.rsqrt(jnp.mean(x * x, axis=-1, keepdims=True) + EPS) * g


def _dot(a, b):
    return jnp.dot(a, b, preferred_element_type=F32)


def _dot_t(a, b):
    return lax.dot_general(a, b, (((1,), (1,)), ((), ())), preferred_element_type=F32)


def _split_bf16(x):
    hi = x.astype(BF16)
    lo = (x - hi.astype(F32)).astype(BF16)
    return hi, lo


COL_TILE = 256


def _conv_in_body(x_ref, g_ref, w_ref, b_ref, o_ref):
    u = _rms(x_ref[...], g_ref[...]).astype(BF16)
    for j in range(D_MODEL // COL_TILE):
        lo, hi = j * COL_TILE, (j + 1) * COL_TILE
        a1 = _dot(u, w_ref[:, lo:hi]) + b_ref[:, lo:hi]
        a2 = _dot(u, w_ref[:, D_MODEL + lo:D_MODEL + hi]) + b_ref[:, D_MODEL + lo:D_MODEL + hi]
        o_ref[:, lo:hi] = a1 * jax.nn.sigmoid(a2)


def conv_in(h, g, w, b, tm):
    m = h.shape[0]
    return pl.pallas_call(
        _conv_in_body,
        out_shape=jax.ShapeDtypeStruct((m, D_MODEL), F32),
        grid=(m // tm,),
        in_specs=[
            pl.BlockSpec((tm, D_MODEL), lambda i: (i, 0)),
            pl.BlockSpec((1, D_MODEL), lambda i: (0, 0)),
            pl.BlockSpec((D_MODEL, 2 * D_MODEL), lambda i: (0, 0)),
            pl.BlockSpec((1, 2 * D_MODEL), lambda i: (0, 0)),
        ],
        out_specs=pl.BlockSpec((tm, D_MODEL), lambda i: (i, 0)),
        compiler_params=_cparams("parallel"),
        name="conv_in",
    )(h, g, w, b)


HALO = 32
DW_ROWS = 32
DW_LANES = 512


def _ln_silu(y, ln_g, ln_b):
    mu = jnp.mean(y, axis=-1, keepdims=True)
    yc = y - mu
    var = jnp.mean(yc * yc, axis=-1, keepdims=True)
    yn = yc * lax.rsqrt(var + EPS) * ln_g + ln_b
    return yn * jax.nn.sigmoid(yn)


def _dwconv_prompt_body(ts, cur_ref, halo_ref, w_ref, bdw_ref, lng_ref, lnb_ref, z_ref, xin_ref, y_ref):
    first = pl.program_id(1) == 0

    @pl.when(first)
    def _():
        xin_ref[0:HALO, :] = jnp.zeros((HALO, D_MODEL), F32)

    @pl.when(jnp.logical_not(first))
    def _():
        xin_ref[0:HALO, :] = halo_ref[...]

    xin_ref[HALO:HALO + ts, :] = cur_ref[...]
    off = HALO - (CONV_WIDTH - 1)
    for rc in range(ts // DW_ROWS):
        for lc in range(D_MODEL // DW_LANES):
            l0, l1 = lc * DW_LANES, (lc + 1) * DW_LANES
            acc = jnp.zeros((DW_ROWS, DW_LANES), F32)
            for k in range(CONV_WIDTH):
                r0 = off + rc * DW_ROWS + k
                acc = acc + xin_ref[r0:r0 + DW_ROWS, l0:l1] * w_ref[k:k + 1, l0:l1]
            y_ref[rc * DW_ROWS:(rc + 1) * DW_ROWS, l0:l1] = acc
    y = y_ref[...] + bdw_ref[...]
    z_ref[...] = _ln_silu(y, lng_ref[...], lnb_ref[...]).astype(BF16)


def dwconv_prompt(glu, n_seq, seq, w_dw, b_dw, ln_g, ln_b, ts=256):
    nt = seq // ts
    hb = ts // HALO
    return pl.pallas_call(
        functools.partial(_dwconv_prompt_body, ts),
        out_shape=jax.ShapeDtypeStruct((n_seq * seq, D_MODEL), BF16),
        grid=(n_seq, nt),
        in_specs=[
            pl.BlockSpec((ts, D_MODEL), lambda b, i: (b * nt + i, 0)),
            pl.BlockSpec((HALO, D_MODEL), lambda b, i: (jnp.maximum((b * nt + i) * hb - 1, 0), 0)),
            pl.BlockSpec((CONV_WIDTH, D_MODEL), lambda b, i: (0, 0)),
            pl.BlockSpec((1, D_MODEL), lambda b, i: (0, 0)),
            pl.BlockSpec((1, D_MODEL), lambda b, i: (0, 0)),
            pl.BlockSpec((1, D_MODEL), lambda b, i: (0, 0)),
        ],
        out_specs=pl.BlockSpec((ts, D_MODEL), lambda b, i: (b * nt + i, 0)),
        scratch_shapes=[pltpu.VMEM((HALO + ts, D_MODEL), F32), pltpu.VMEM((ts, D_MODEL), F32)],
        compiler_params=_cparams("parallel", "parallel"),
        name="dwconv_prompt",
    )(glu, glu, w_dw, b_dw, ln_g, ln_b)


def _dwconv_sample_body(n_new, st_ref, glu_ref, w_ref, bdw_ref, lng_ref, lnb_ref, z_ref, ns_ref):
    n_st = CONV_WIDTH - 1
    rows = [st_ref[j] for j in range(n_st)] + [glu_ref[:, j, :] for j in range(n_new)]
    for t in range(n_new):
        acc = rows[t] * w_ref[0:1, :]
        for k in range(1, CONV_WIDTH):
            acc = acc + rows[t + k] * w_ref[k:k + 1, :]
        y = acc + bdw_ref[...]
        z_ref[:, t, :] = _ln_silu(y, lng_ref[...], lnb_ref[...])
    for j in range(n_st):
        ns_ref[j] = rows[j + n_new]


def dwconv_sample(state, layer, glu, w_dw, b_dw, ln_g, ln_b, bb=8):
    nb, n_new, _ = glu.shape
    n_st = CONV_WIDTH - 1
    vec = pl.BlockSpec((1, D_MODEL), lambda i: (0, 0))
    return pl.pallas_call(
        functools.partial(_dwconv_sample_body, n_new),
        out_shape=(jax.ShapeDtypeStruct((nb, n_new, D_MODEL), F32),
                   jax.ShapeDtypeStruct((n_st, nb, D_MODEL), F32)),
        grid=(nb // bb,),
        in_specs=[
            pl.BlockSpec((n_st, bb, D_MODEL), lambda i: (layer, i, 0)),
            pl.BlockSpec((bb, n_new, D_MODEL), lambda i: (i, 0, 0)),
            pl.BlockSpec((CONV_WIDTH, D_MODEL), lambda i: (0, 0)),
            vec, vec, vec,
        ],
        out_specs=(pl.BlockSpec((bb, n_new, D_MODEL), lambda i: (i, 0, 0)),
                   pl.BlockSpec((n_st, bb, D_MODEL), lambda i: (0, i, 0))),
        compiler_params=_cparams("parallel"),
        name="dwconv_sample",
    )(state, glu, w_dw, b_dw, ln_g, ln_b)


def _out_proj_body(z_ref, w_ref, b_ref, g_ref, h_ref, o_ref):
    m = _dot(z_ref[...].astype(BF16), w_ref[...]) + b_ref[...]
    o_ref[...] = h_ref[...] + _rms(m, g_ref[...])


def out_proj_res(z, w, b, g, h, tm):
    m = h.shape[0]
    vec = pl.BlockSpec((1, D_MODEL), lambda i: (0, 0))
    row = pl.BlockSpec((tm, D_MODEL), lambda i: (i, 0))
    return pl.pallas_call(
        _out_proj_body,
        out_shape=jax.ShapeDtypeStruct((m, D_MODEL), F32),
        grid=(m // tm,),
        in_specs=[row, pl.BlockSpec((D_MODEL, D_MODEL), lambda i: (0, 0)), vec, vec, row],
        out_specs=row,
        compiler_params=_cparams("parallel"),
        name="out_proj_res",
    )(z, w, b, g, h)


FF_CHUNK = 1024


def _ffn_body(h_ref, p_ref, gpre_ref, gpost_ref, win_ref, wout_ref, wpg_ref, wpp_ref, o_ref, u_ref, acc_ref):
    j = pl.program_id(1)

    @pl.when(j == 0)
    def _():
        u_ref[...] = _rms(h_ref[...], gpre_ref[...]).astype(BF16)
        acc_ref[...] = jnp.zeros_like(acc_ref)

    a = _dot(u_ref[...], win_ref[...])
    a = jnp.square(jnp.maximum(a, 0.0)).astype(BF16)
    acc_ref[...] += _dot(a, wout_ref[...])

    @pl.when(j == pl.num_programs(1) - 1)
    def _():
        h2 = h_ref[...] + _rms(acc_ref[...], gpost_ref[...])
        gate = jax.nn.sigmoid(_dot(h2.astype(BF16), wpg_ref[...]))
        proj = _dot(p_ref[...].astype(BF16), wpp_ref[...])
        o_ref[...] = h2 + gate * proj


def ffn_ple(h, p, g_pre, g_post, w_in, w_out, w_pg, w_pp, tm):
    m = h.shape[0]
    ple = p.shape[1]
    vec = pl.BlockSpec((1, D_MODEL), lambda i, j: (0, 0))
    row = pl.BlockSpec((tm, D_MODEL), lambda i, j: (i, 0))
    return pl.pallas_call(
        _ffn_body,
        out_shape=jax.ShapeDtypeStruct((m, D_MODEL), F32),
        grid=(m // tm, D_FF // FF_CHUNK),
        in_specs=[
            row,
            pl.BlockSpec((tm, ple), lambda i, j: (i, 0)),
            vec, vec,
            pl.BlockSpec((D_MODEL, FF_CHUNK), lambda i, j: (0, j)),
            pl.BlockSpec((FF_CHUNK, D_MODEL), lambda i, j: (j, 0)),
            pl.BlockSpec((D_MODEL, D_MODEL), lambda i, j: (0, 0)),
            pl.BlockSpec((ple, D_MODEL), lambda i, j: (0, 0)),
        ],
        out_specs=row,
        scratch_shapes=[pltpu.VMEM((tm, D_MODEL), BF16), pltpu.VMEM((tm, D_MODEL), F32)],
        compiler_params=_cparams("parallel", "arbitrary"),
        name="ffn_ple",
    )(h, p, g_pre, g_post, w_in, w_out, w_pg, w_pp)


NSA_W_COLS = Q_COLS + KV_COLS + GATE_PAD


def _pack_groups(k, v, out_ref):
    low = lax.broadcasted_iota(jnp.int32, (k.shape[0], 128), 1) < HEAD_DIM
    for pair in range(KV_GROUPS // 2):
        kp = k[:, pair * 128:(pair + 1) * 128]
        vp = v[:, pair * 128:(pair + 1) * 128]
        out_ref[0, 2 * pair] = jnp.where(low, kp, pltpu.roll(vp, HEAD_DIM, 1)).astype(BF16)
        out_ref[0, 2 * pair + 1] = jnp.where(low, pltpu.roll(kp, HEAD_DIM, 1), vp).astype(BF16)


def _nsa_in_body(transposed, x_ref, g_ref, w_ref, bg_ref, cos_ref, sin_ref, qq_ref, *out_refs):
    if transposed:
        kc_ref, vc_ref = out_refs[0:2]
        t_refs = out_refs[2:8]
        outs = (kc_ref, vc_ref, None, None, None, None)
        packed_refs = out_refs[8:10]
        kvb_ref = None
    else:
        outs = out_refs[0:6]
        t_refs = None
        kvb_ref = out_refs[6]
    gate_ref = out_refs[-1]
    u = _rms(x_ref[...], g_ref[...]).astype(BF16)
    tm = u.shape[0]
    cos = jnp.concatenate([cos_ref[...], cos_ref[...]], axis=1)
    sin = jnp.concatenate([sin_ref[...], sin_ref[...]], axis=1)
    lane = lax.broadcasted_iota(jnp.int32, (tm, KV_W), 1)
    first_half = (lane & (HEAD_DIM - 1)) < (HEAD_DIM // 2)

    def rope(x):
        rot = jnp.where(first_half, pltpu.roll(x, KV_W - HEAD_DIM // 2, 1), pltpu.roll(x, HEAD_DIM // 2, 1))
        return x * cos + rot * sin

    scale = HEAD_DIM ** -0.5 * LOG2E
    for j in range(Q_COLS // KV_W):
        q = _dot(u, w_ref[:, j * KV_W:(j + 1) * KV_W]) * scale
        qq_ref[:, j * KV_W:(j + 1) * KV_W] = q.astype(BF16)
        qq_ref[:, Q_COLS + j * KV_W:Q_COLS + (j + 1) * KV_W] = rope(q).astype(BF16)
    for j in range(2 * N_BRANCH):
        x = _dot(u, w_ref[:, Q_COLS + j * KV_W:Q_COLS + (j + 1) * KV_W])
        if j in (2, 4):
            x = rope(x)
        if outs[j] is not None:
            outs[j][...] = x
        if transposed:
            t_refs[j][0] = jnp.transpose(x)
            if j in (2, 4):
                k_rows = x
            if j in (3, 5):
                _pack_groups(k_rows, x, packed_refs[(j - 3) // 2])
        elif j >= 2:
            kvb_ref[:, (j - 2) * KV_W:(j - 1) * KV_W] = x.astype(BF16)
    gl = _dot(u, w_ref[:, Q_COLS + KV_COLS:]) + bg_ref[...]
    gate_ref[...] = jax.nn.sigmoid(gl)


def nsa_in(h, g, w, bg, cos_t, sin_t, pos_map, tm, seq=None):
    m = h.shape[0]
    kv_spec = pl.BlockSpec((tm, KV_W), lambda i: (i, 0))
    kv_shape = jax.ShapeDtypeStruct((m, KV_W), F32)
    if seq is None:
        kv_shapes = (kv_shape,) * 6 + (jax.ShapeDtypeStruct((m, 4 * KV_W), BF16),)
        kv_specs = (kv_spec,) * 6 + (pl.BlockSpec((tm, 4 * KV_W), lambda i: (i, 0)),)
    else:
        nt = seq // tm
        t_shape = jax.ShapeDtypeStruct((m // seq, KV_W, seq), F32)
        t_spec = pl.BlockSpec((1, KV_W, tm), lambda i: (i // nt, 0, i % nt))
        p_shape = jax.ShapeDtypeStruct((m // seq, KV_GROUPS, seq, 128), BF16)
        p_spec = pl.BlockSpec((1, KV_GROUPS, tm, 128), lambda i: (i // nt, 0, i % nt, 0))
        kv_shapes = (kv_shape,) * 2 + (t_shape,) * 6 + (p_shape,) * 2
        kv_specs = (kv_spec,) * 2 + (t_spec,) * 6 + (p_spec,) * 2
    return pl.pallas_call(
        functools.partial(_nsa_in_body, seq is not None),
        out_shape=(jax.ShapeDtypeStruct((m, 2 * Q_COLS), BF16),) + kv_shapes + (
                   jax.ShapeDtypeStruct((m, GATE_PAD), F32),),
        grid=(m // tm,),
        in_specs=[
            pl.BlockSpec((tm, D_MODEL), lambda i: (i, 0)),
            pl.BlockSpec((1, D_MODEL), lambda i: (0, 0)),
            pl.BlockSpec((D_MODEL, NSA_W_COLS), lambda i: (0, 0)),
            pl.BlockSpec((1, GATE_PAD), lambda i: (0, 0)),
            pl.BlockSpec((tm, 2 * HEAD_DIM), lambda i: (pos_map(i), 0)),
            pl.BlockSpec((tm, 2 * HEAD_DIM), lambda i: (pos_map(i), 0)),
        ],
        out_specs=(pl.BlockSpec((tm, 2 * Q_COLS), lambda i: (i, 0)),) + kv_specs + (
                   pl.BlockSpec((tm, GATE_PAD), lambda i: (i, 0)),),
        compiler_params=_cparams("parallel"),
        name="nsa_in",
    )(h, g, w, bg, cos_t, sin_t)


CHUNK_W = CMP_STRIDE * HEAD_DIM
TOK_PAIRS = CMP_STRIDE // 2


def _compress_one(load_t, nc, n_out, w1_ref, pe_ref, w2_ref, out_ref, c_ref, ab_ref):
    lane = lax.broadcasted_iota(jnp.int32, (nc, 2 * HEAD_DIM), 1)
    low = lane < HEAD_DIM
    for tp in range(TOK_PAIRS):
        for half in range(KV_GROUPS // 2):
            a = load_t(2 * tp, half)
            b = load_t(2 * tp + 1, half)
            even = jnp.where(low, a, pltpu.roll(b, HEAD_DIM, 1))
            odd = jnp.where(low, pltpu.roll(a, HEAD_DIM, 1), b)
            g0 = 2 * half
            c_ref[g0 * nc:(g0 + 1) * nc, tp * 128:(tp + 1) * 128] = even.astype(BF16)
            c_ref[(g0 + 1) * nc:(g0 + 2) * nc, tp * 128:(tp + 1) * 128] = odd.astype(BF16)
    ab_ref[0:KV_GROUPS * nc, :] = _dot(c_ref[...], w1_ref[...])
    ab_ref[KV_GROUPS * nc:KV_GROUPS * nc + 8, :] = jnp.zeros((8, 2 * CMP_HIDDEN), F32)
    pe2 = _dot(pe_ref[...].astype(BF16), w1_ref[...])
    pe_c = pe2[0:1, 0:CMP_HIDDEN] + pe2[1:2, CMP_HIDDEN:2 * CMP_HIDDEN]
    hs = []
    for g in range(KV_GROUPS):
        top = ab_ref[g * nc:g * nc + n_out, 0:CMP_HIDDEN]
        bot = ab_ref[g * nc + 1:g * nc + 1 + n_out, CMP_HIDDEN:2 * CMP_HIDDEN]
        hs.append(jax.nn.gelu(top + bot + pe_c).astype(BF16))
    if out_ref is None:
        return hs
    hcat = jnp.concatenate(hs, axis=1)
    out_ref[0] = _dot(hcat, w2_ref[...]).astype(BF16)


def _compress_prompt_body(nc, k0_ref, k1_ref, v0_ref, v1_ref, wk1_ref, pek_ref, wv1_ref, pev_ref, w2_ref,
                          ckv_ref, c_ref, ab_ref):
    def loader(halves):
        return lambda t, half: halves[half][pl.ds(t, nc, stride=CMP_STRIDE), :]

    hk = _compress_one(loader((k0_ref, k1_ref)), nc, nc, wk1_ref, pek_ref, None, None, c_ref, ab_ref)
    hv = _compress_one(loader((v0_ref, v1_ref)), nc, nc, wv1_ref, pev_ref, None, None, c_ref, ab_ref)
    for g in range(KV_GROUPS):
        ckv_ref[0, g] = _dot(jnp.concatenate([hk[g], hv[g]], axis=1), w2_ref[...]).astype(BF16)


def _compress_weight_specs(idx):
    return [pl.BlockSpec((CHUNK_W, 2 * CMP_HIDDEN), idx),
            pl.BlockSpec((8, CHUNK_W), idx),
            pl.BlockSpec((KV_GROUPS * CMP_HIDDEN, KV_W), idx)]


def compress_prompt(kc, vc, n_seq, seq, wk, wv, w2_packed):
    nc = seq // CMP_STRIDE
    idx = lambda b: (0, 0)
    halves = [pl.BlockSpec((seq, 128), lambda b: (b, 0)), pl.BlockSpec((seq, 128), lambda b: (b, 1))]
    w1_specs = _compress_weight_specs(idx)[0:2]
    return pl.pallas_call(
        functools.partial(_compress_prompt_body, nc),
        out_shape=jax.ShapeDtypeStruct((n_seq, KV_GROUPS, nc, 128), BF16),
        grid=(n_seq,),
        in_specs=halves + halves + w1_specs + w1_specs + [pl.BlockSpec(w2_packed.shape, idx)],
        out_specs=pl.BlockSpec((1, KV_GROUPS, nc, 128), lambda b: (b, 0, 0, 0)),
        scratch_shapes=[pltpu.VMEM((KV_GROUPS * nc, CHUNK_W), BF16),
                        pltpu.VMEM((KV_GROUPS * nc + 8, 2 * CMP_HIDDEN), F32)],
        compiler_params=_cparams("parallel"),
        name="compress_prompt",
    )(kc, kc, vc, vc, wk[0], wk[1], wv[0], wv[1], w2_packed)


def _compress_sample_body(n_pages, page, nc, n_out, pt_ref, *refs):
    k_pages = refs[0:n_pages]
    v_pages = refs[n_pages:2 * n_pages]
    k_new, v_new = refs[2 * n_pages:2 * n_pages + 2]
    wk1_ref, pek_ref, wk2_ref, wv1_ref, pev_ref, wv2_ref = refs[2 * n_pages + 2:2 * n_pages + 8]
    ck_ref, cv_ref, c_ref, ab_ref, x0_ref, x1_ref = refs[2 * n_pages + 8:]
    past = n_pages * page
    n_rows = nc * CMP_STRIDE
    halves = (x0_ref, x1_ref)

    def run(pages, new_ref, w1_ref, pe_ref, w2_ref, out_ref):
        for p, page_ref in enumerate(pages):
            for half in range(2):
                halves[half][p * page:(p + 1) * page, :] = jnp.transpose(page_ref[0, half * 128:(half + 1) * 128, :])
        for half in range(2):
            halves[half][past:past + CMP_STRIDE, :] = new_ref[0, :, half * 128:(half + 1) * 128]
            halves[half][past + CMP_STRIDE:n_rows, :] = jnp.zeros((n_rows - past - CMP_STRIDE, 128), F32)
        load_t = lambda t, half: halves[half][pl.ds(t, nc, stride=CMP_STRIDE), :]
        _compress_one(load_t, nc, n_out, w1_ref, pe_ref, w2_ref, out_ref, c_ref, ab_ref)

    run(k_pages, k_new, wk1_ref, pek_ref, wk2_ref, ck_ref)
    run(v_pages, v_new, wv1_ref, pev_ref, wv2_ref, cv_ref)


def compress_sample(pool_k, pool_v, page0, page_table, k_new, v_new, wk, wv):
    nb, n_pages = page_table.shape
    page = pool_k.shape[2]
    assert page == 128
    n_chunk = n_pages * (page // CMP_STRIDE) + 1
    n_out = n_chunk - 1
    nc = -(-(n_chunk + 1) // 16) * 16
    idx = lambda b, pt: (0, 0)
    page_specs = [pl.BlockSpec((1, KV_W, page), functools.partial(lambda p, b, pt: (page0 + pt[b * n_pages + p], 0, 0), p))
                  for p in range(n_pages)]
    new_spec = pl.BlockSpec((1, CMP_STRIDE, KV_W), lambda b, pt: (b, 0, 0))
    out = jax.ShapeDtypeStruct((nb, n_out, KV_W), BF16)
    grid_spec = pltpu.PrefetchScalarGridSpec(
        num_scalar_prefetch=1,
        grid=(nb,),
        in_specs=page_specs * 2 + [new_spec, new_spec]
        + _compress_weight_specs(idx) + _compress_weight_specs(idx),
        out_specs=(pl.BlockSpec((1, n_out, KV_W), lambda b, pt: (b, 0, 0)),
                   pl.BlockSpec((1, n_out, KV_W), lambda b, pt: (b, 0, 0))),
        scratch_shapes=[pltpu.VMEM((KV_GROUPS * nc, CHUNK_W), BF16),
                        pltpu.VMEM((KV_GROUPS * nc + 8, 2 * CMP_HIDDEN), F32),
                        pltpu.VMEM((nc * CMP_STRIDE, 128), F32),
                        pltpu.VMEM((nc * CMP_STRIDE, 128), F32)],
    )
    return pl.pallas_call(
        functools.partial(_compress_sample_body, n_pages, page, nc, n_out),
        out_shape=(out, out),
        grid_spec=grid_spec,
        compiler_params=_cparams("parallel"),
        name="compress_sample",
    )(page_table.reshape(-1), *([pool_k] * n_pages), *([pool_v] * n_pages), k_new, v_new, *wk, *wv)


def _group_masks():
    lane = lax.broadcasted_iota(jnp.int32, (1, KV_W), 1)
    return [(lane // HEAD_DIM) == g for g in range(KV_GROUPS)]


def _stack_q(q, gmask):
    if q.shape[0] % 16:
        q = q.astype(F32)
    zero = jnp.zeros((q.shape[0], KV_W), q.dtype)
    return jnp.concatenate([jnp.where(gmask[g], q[:, r * KV_W:(r + 1) * KV_W], zero)
                            for g in range(KV_GROUPS) for r in range(GROUP_SIZE)], axis=0).astype(BF16)


def _unstack_o(o, tq, gmask):
    cols = []
    for r in range(GROUP_SIZE):
        acc = jnp.zeros((tq, KV_W), F32)
        for g in range(KV_GROUPS):
            row = (g * GROUP_SIZE + r) * tq
            acc = acc + jnp.where(gmask[g], o[row:row + tq, :], 0.0)
        cols.append(acc)
    return jnp.concatenate(cols, axis=1)


def _rep_rows(x, tq):
    return jnp.concatenate([x[g * tq:(g + 1) * tq] for g in range(KV_GROUPS) for _ in range(GROUP_SIZE)], axis=0)


def _softmax_rows(s, mask):
    s = jnp.where(mask, s, NEG)
    m = jnp.max(s, axis=1, keepdims=True)
    e = jnp.where(mask, jnp.exp2(s - m), 0.0)
    d = jnp.sum(e, axis=1, keepdims=True)
    return e * (1.0 / jnp.where(d > 0, d, 1.0))


def _cmp_branch(qc, ck, cv, qpos0, tq):
    rows, n = qc.shape[0], ck.shape[0]
    s = _dot_t(qc, ck)
    col = lax.broadcasted_iota(jnp.int32, (rows, n), 1)
    qp = qpos0 + (lax.broadcasted_iota(jnp.int32, (rows, n), 0) & (tq - 1))
    p = _softmax_rows(s, col * CMP_STRIDE + (CMP_BLOCK - 1) <= qp)
    o = _dot(p.astype(BF16), cv)
    psum = []
    for g in range(KV_GROUPS):
        acc = p[g * GROUP_SIZE * tq:(g * GROUP_SIZE + 1) * tq]
        for r in range(1, GROUP_SIZE):
            acc = acc + p[(g * GROUP_SIZE + r) * tq:(g * GROUP_SIZE + r + 1) * tq]
        psum.append(acc)
    return o, jnp.concatenate(psum, axis=0)


def _select_bias(psum, cov_t, qpos0, tq, n_live, score_ref):
    hi, lo = _split_bf16(psum)
    imp = _dot_t(cov_t, hi) + _dot_t(cov_t, lo)
    shape = imp.shape
    blk = lax.broadcasted_iota(jnp.int32, shape, 0)
    cur = (qpos0 + (lax.broadcasted_iota(jnp.int32, shape, 1) & (tq - 1))) // SEL_BLOCK
    forced = (blk == 0) | (blk == cur) | (blk == cur - 1)
    score = jnp.where(blk > cur, -jnp.inf, jnp.where(forced, jnp.inf, imp))
    score_ref[...] = score

    def count(k, rank):
        sk = jnp.broadcast_to(score_ref[pl.ds(k, 1), :], shape)
        ge = jnp.where(sk >= score, 1.0, 0.0)
        gt = jnp.where(sk > score, 1.0, 0.0)
        return rank + jnp.where(blk > k, ge, gt)

    rank = lax.fori_loop(0, n_live, count, jnp.zeros(shape, F32))
    bias_t = jnp.where(rank < N_SELECT, 0.0, NEG)
    bias_t = jnp.concatenate([bias_t, jnp.zeros(shape, F32)], axis=0)
    return jnp.transpose(bias_t).astype(BF16)


def _gate_mix(gate, gexp_ref, outs):
    hi, lo = _split_bf16(gate)
    total = None
    for br, o in enumerate(outs):
        ge = _dot(hi, gexp_ref[br]) + _dot(lo, gexp_ref[br])
        total = ge * o if total is None else total + ge * o
    return total


TQ_P = 256
KEY_CHUNK = 512
SM_ROWS = 32


def _attn_prompt_body(seq, qq_ref, gate_ref, ckv_ref, kvs_ref, kvw_ref, cov_ref, et_ref, gexp_ref,
                      o_ref, score_ref, s_ref, p_ref, b_ref, psum_ref, m_ref, l_ref, acc_ref, alpha_ref):
    tq = TQ_P
    rows = GROUP_SIZE * tq
    n = pl.program_id(1)
    q0 = n * tq
    low = lax.broadcasted_iota(jnp.int32, (tq, 128), 1) < HEAD_DIM

    def q_rows(off, g):
        pieces = []
        for r in range(GROUP_SIZE):
            c0 = off + r * KV_W + (g // 2) * 128
            pair = qq_ref[:, c0:c0 + 128].astype(F32)
            if g % 2:
                pair = pltpu.roll(pair, HEAD_DIM, 1)
            pieces.append(jnp.where(low, pair, 0.0))
        return jnp.concatenate(pieces, axis=0).astype(BF16)

    def blocks(fn):
        for r0 in range(0, rows, SM_ROWS):
            fn(r0, r0 % tq)

    def block_iotas(width):
        return (lax.broadcasted_iota(jnp.int32, (SM_ROWS, width), 0),
                lax.broadcasted_iota(jnp.int32, (SM_ROWS, width), 1))

    def lanes(x, width):
        return jnp.concatenate([x] * (width // 128), axis=1)

    qc = [q_rows(0, g) for g in range(KV_GROUPS)]
    qr = [q_rows(Q_COLS, g) for g in range(KV_GROUPS)]


    n_cmp = ckv_ref.shape[2]
    psum_ref[...] = jnp.zeros_like(psum_ref)
    o_c = []
    for g in range(KV_GROUPS):
        ckv = ckv_ref[0, g]
        s_ref[:, 0:n_cmp] = _dot_t(qc[g], ckv)

        def cmp_block(r0, qoff, g=g):
            row, col = block_iotas(n_cmp)
            p = _softmax_rows(s_ref[pl.ds(r0, SM_ROWS), 0:n_cmp],
                              col * CMP_STRIDE + (CMP_BLOCK - 1) <= q0 + qoff + row)
            p_ref[pl.ds(r0, SM_ROWS), 0:n_cmp] = p.astype(BF16)
            psum_ref[pl.ds(g * tq + qoff, SM_ROWS), :] += p

        blocks(cmp_block)
        o_c.append(_dot(p_ref[:, 0:n_cmp], ckv))
    bias = _select_bias(psum_ref[...], cov_ref[...], q0, tq,
                        (q0 + tq - 1) // SEL_BLOCK + 1, score_ref)

    m_ref[...] = jnp.full(m_ref.shape, NEG, F32)
    l_ref[...] = jnp.zeros_like(l_ref)
    acc_ref[...] = jnp.zeros_like(acc_ref)

    def chunk(c, carry, causal):
        k0 = pl.multiple_of(c * KEY_CHUNK, KEY_CHUNK)
        b_ref[...] = _dot_t(bias, et_ref[pl.ds(k0, KEY_CHUNK), :])
        for g in range(KV_GROUPS):
            kv = kvs_ref[0, g, pl.ds(k0, KEY_CHUNK), :]
            s_ref[:, 0:KEY_CHUNK] = _dot_t(qr[g], kv)

            def sel_block(r0, qoff, g=g):
                t = s_ref[pl.ds(r0, SM_ROWS), 0:KEY_CHUNK] + b_ref[pl.ds(g * tq + qoff, SM_ROWS), :]
                if causal:
                    row, col = block_iotas(KEY_CHUNK)
                    t = jnp.where(k0 + col <= q0 + qoff + row, t, NEG)
                m_old = m_ref[g, pl.ds(r0, SM_ROWS), :]
                m_new = jnp.maximum(m_old, jnp.max(t, axis=1, keepdims=True))
                alpha = jnp.exp2(m_old - m_new)
                p = jnp.exp2(t - lanes(m_new, KEY_CHUNK))
                l_ref[g, pl.ds(r0, SM_ROWS), :] = (alpha * l_ref[g, pl.ds(r0, SM_ROWS), :]
                                                   + jnp.sum(p, axis=1, keepdims=True))
                m_ref[g, pl.ds(r0, SM_ROWS), :] = m_new
                alpha_ref[pl.ds(r0, SM_ROWS), :] = alpha
                p_ref[pl.ds(r0, SM_ROWS), 0:KEY_CHUNK] = p.astype(BF16)

            blocks(sel_block)
            acc_ref[g] = alpha_ref[...] * acc_ref[g] + _dot(p_ref[:, 0:KEY_CHUNK], kv)
        return carry

    diag = q0 // KEY_CHUNK
    lax.fori_loop(0, diag, functools.partial(chunk, causal=False), 0)
    chunk(diag, 0, True)
    o_s = [acc_ref[g] * (1.0 / l_ref[g]) for g in range(KV_GROUPS)]

    nk = WINDOW + tq
    base = pl.multiple_of(jnp.maximum(q0 - WINDOW, 0), tq)
    o_w = []
    for g in range(KV_GROUPS):
        kv = kvw_ref[0, g, pl.ds(base, nk), :]
        s_ref[:, 0:nk] = _dot_t(qr[g], kv)

        def win_block(r0, qoff):
            row, col = block_iotas(nk)
            kpos = base + col
            qp = q0 + qoff + row
            t = s_ref[pl.ds(r0, SM_ROWS), 0:nk]
            t = jnp.where(kpos <= qp, jnp.where(kpos > qp - WINDOW, t, NEG), NEG)
            p = jnp.exp2(t - jnp.max(t, axis=1, keepdims=True))
            alpha_ref[pl.ds(r0, SM_ROWS), :] = jnp.broadcast_to(jnp.sum(p, axis=1, keepdims=True), (SM_ROWS, 128))
            p_ref[pl.ds(r0, SM_ROWS), 0:nk] = p.astype(BF16)

        blocks(win_block)
        o_w.append(_dot(p_ref[:, 0:nk], kv) * (1.0 / alpha_ref[...]))

    def to_heads(o):
        tiles = []
        for r in range(GROUP_SIZE):
            for pair in range(KV_GROUPS // 2):
                even = pltpu.roll(o[2 * pair][r * tq:(r + 1) * tq], HEAD_DIM, 1)
                tiles.append(jnp.where(low, even, o[2 * pair + 1][r * tq:(r + 1) * tq]))
        return jnp.concatenate(tiles, axis=1)

    outs = [to_heads(o) for o in (o_c, o_s, o_w)]
    o_ref[...] = _gate_mix(gate_ref[...], gexp_ref, outs).astype(BF16)


def attn_prompt(qq, gate, ckv, kvs, kvw, n_seq, seq, cov_t, e_t, gexp):
    tq = TQ_P
    nt = seq // tq
    n_cmp = ckv.shape[2]
    rows = GROUP_SIZE * tq
    max_keys = max(WINDOW + tq, KEY_CHUNK, n_cmp)
    return pl.pallas_call(
        functools.partial(_attn_prompt_body, seq),
        out_shape=jax.ShapeDtypeStruct((n_seq * seq, Q_COLS), BF16),
        grid=(n_seq, nt),
        in_specs=[
            pl.BlockSpec((tq, 2 * Q_COLS), lambda b, i: (b * nt + i, 0)),
            pl.BlockSpec((tq, GATE_PAD), lambda b, i: (b * nt + i, 0)),
            pl.BlockSpec((1, KV_GROUPS, n_cmp, 128), lambda b, i: (b, 0, 0, 0)),
            pl.BlockSpec((1, KV_GROUPS, seq, 128), lambda b, i: (b, 0, 0, 0), pipeline_mode=pl.Buffered(1)),
            pl.BlockSpec((1, KV_GROUPS, seq, 128), lambda b, i: (b, 0, 0, 0), pipeline_mode=pl.Buffered(1)),
            pl.BlockSpec(cov_t.shape, lambda b, i: (0, 0)),
            pl.BlockSpec(e_t.shape, lambda b, i: (0, 0), pipeline_mode=pl.Buffered(1)),
            pl.BlockSpec(gexp.shape, lambda b, i: (0, 0, 0)),
        ],
        out_specs=pl.BlockSpec((tq, Q_COLS), lambda b, i: (b * nt + i, 0)),
        scratch_shapes=[
            pltpu.VMEM((MAX_SEL_BLOCKS, KV_GROUPS * tq), F32),
            pltpu.VMEM((rows, max_keys), F32),
            pltpu.VMEM((rows, max_keys), BF16),
            pltpu.VMEM((KV_GROUPS * tq, KEY_CHUNK), F32),
            pltpu.VMEM((KV_GROUPS * tq, n_cmp), F32),
            pltpu.VMEM((KV_GROUPS, rows, 128), F32),
            pltpu.VMEM((KV_GROUPS, rows, 128), F32),
            pltpu.VMEM((KV_GROUPS, rows, 128), F32),
            pltpu.VMEM((rows, 128), F32),
        ],
        compiler_params=_cparams("parallel", "parallel"),
        name="attn_prompt",
    )(qq, gate, ckv, kvs, kvw, cov_t, e_t, gexp)


TQ_S = 8
NEW_PAD = 128


def _attn_sample_body(n_pages, page, n_new, pt_ref, *refs):
    ks_pages = refs[0:n_pages]
    vs_pages = refs[n_pages:2 * n_pages]
    (qq_ref, gate_ref, ck_ref, cv_ref, kvn_ref, kwn_ref, vwn_ref, kwb_ref, vwb_ref,
     cov_ref, et_ref, gexp_ref, eye_ref, o_ref, kwo_ref, vwo_ref, score_ref) = refs[2 * n_pages:]
    tq = TQ_S
    rows = KV_GROUPS * GROUP_SIZE * tq
    past = n_pages * page
    gmask = _group_masks()
    qc = _stack_q(qq_ref[0, :, 0:Q_COLS], gmask)
    qr = _stack_q(qq_ref[0, :, Q_COLS:2 * Q_COLS], gmask)

    o_c, psum = _cmp_branch(qc, ck_ref[0], cv_ref[0], past, tq)
    n_live = (past + tq - 1) // SEL_BLOCK + 1
    bias = _select_bias(psum, cov_ref[...], past, tq, n_live, score_ref)

    pad = jnp.zeros((NEW_PAD - tq, KV_W), BF16)
    eye = eye_ref[...]

    def new_cols(col):
        x = jnp.concatenate([kvn_ref[0, :, col * KV_W:(col + 1) * KV_W], pad], axis=0)
        return _dot_t(eye, x).astype(BF16)

    k_all = jnp.concatenate([p[0].astype(BF16) for p in ks_pages] + [new_cols(0)], axis=1)
    v_all = jnp.concatenate([p[0].astype(BF16) for p in vs_pages] + [new_cols(1)], axis=1)
    nk = past + NEW_PAD
    s = _dot(qr, k_all) + _rep_rows(_dot_t(bias, et_ref[...]), tq)
    kpos = lax.broadcasted_iota(jnp.int32, (rows, nk), 1)
    qp = past + (lax.broadcasted_iota(jnp.int32, (rows, nk), 0) & (tq - 1))
    o_s = _dot_t(_softmax_rows(s, kpos <= qp).astype(BF16), v_all)

    wb = kwb_ref.shape[2]
    kw_all = jnp.concatenate([kwb_ref[0].astype(BF16), new_cols(2)], axis=1)
    vw_all = jnp.concatenate([vwb_ref[0].astype(BF16), new_cols(3)], axis=1)
    nw = wb + NEW_PAD
    s = _dot(qr, kw_all)
    kpos = (past - wb) + lax.broadcasted_iota(jnp.int32, (rows, nw), 1)
    qp = past + (lax.broadcasted_iota(jnp.int32, (rows, nw), 0) & (tq - 1))
    mask = (kpos >= 0) & (kpos <= qp) & (kpos > qp - WINDOW)
    o_w = _dot_t(_softmax_rows(s, mask).astype(BF16), vw_all)

    outs = [_unstack_o(o, tq, gmask) for o in (o_c, o_s, o_w)]
    o_ref[0] = _gate_mix(gate_ref[0], gexp_ref, outs).astype(BF16)

    last = lax.broadcasted_iota(jnp.int32, (KV_W, 128), 1) >= 128 - n_new
    fill = jnp.zeros((128 - tq, KV_W), F32)
    for new_ref, buf_ref, out_ref in ((kwn_ref, kwb_ref, kwo_ref), (vwn_ref, vwb_ref, vwo_ref)):
        tail = jnp.concatenate([fill, pltpu.roll(new_ref[0], tq - n_new, 0)], axis=0)
        hi, lo = _split_bf16(tail)
        tail_t = _dot_t(eye, hi) + _dot_t(eye, lo)
        shifted = pltpu.roll(buf_ref[0], wb - n_new, 1)
        out_ref[0, :, 0:wb - 128] = shifted[:, 0:wb - 128]
        out_ref[0, :, wb - 128:wb] = jnp.where(last, tail_t, shifted[:, wb - 128:wb])


def attn_sample(pool_ks, pool_vs, page0, page_table, qq, gate, ck, cv, kvn, kwn, vwn, kwb, vwb, win0, n_new,
                cov_t, e_t, gexp, eye):
    nb, n_pages = page_table.shape
    page = pool_ks.shape[2]
    wb = kwb.shape[2]
    n_cmp = ck.shape[1]
    assert page == 128 and wb % 128 == 0

    def page_spec(p):
        return pl.BlockSpec((1, KV_W, page), lambda b, pt: (page0 + pt[b * n_pages + p], 0, 0))

    def bspec(shape):
        return pl.BlockSpec((1,) + shape, lambda b, pt: (b, 0, 0))

    def cspec(shape):
        return pl.BlockSpec(shape, lambda b, pt: (0,) * len(shape))

    win_spec = pl.BlockSpec((1, KV_W, wb), lambda b, pt: (win0 + b, 0, 0))
    grid_spec = pltpu.PrefetchScalarGridSpec(
        num_scalar_prefetch=1,
        grid=(nb,),
        in_specs=[page_spec(p) for p in range(n_pages)] * 2 + [
            bspec((TQ_S, 2 * Q_COLS)), bspec((TQ_S, GATE_PAD)), bspec((n_cmp, KV_W)), bspec((n_cmp, KV_W)),
            bspec((TQ_S, 4 * KV_W)), bspec((TQ_S, KV_W)), bspec((TQ_S, KV_W)),
            win_spec, win_spec,
            cspec(cov_t.shape), cspec(e_t.shape), cspec(gexp.shape), cspec(eye.shape),
        ],
        out_specs=(bspec((TQ_S, Q_COLS)), bspec((KV_W, wb)), bspec((KV_W, wb))),
        scratch_shapes=[pltpu.VMEM((MAX_SEL_BLOCKS, KV_GROUPS * TQ_S), F32)],
    )
    return pl.pallas_call(
        functools.partial(_attn_sample_body, n_pages, page, n_new),
        out_shape=(jax.ShapeDtypeStruct((nb, TQ_S, Q_COLS), BF16),
                   jax.ShapeDtypeStruct((nb, KV_W, wb), F32),
                   jax.ShapeDtypeStruct((nb, KV_W, wb), F32)),
        grid_spec=grid_spec,
        compiler_params=_cparams("parallel"),
        name="attn_sample",
    )(page_table.reshape(-1), *([pool_ks] * n_pages), *([pool_vs] * n_pages),
      qq, gate, ck, cv, kvn, kwn, vwn, kwb, vwb, cov_t, e_t, gexp, eye)


def _rope_tables(pos):
    half = HEAD_DIM // 2
    inv = ROPE_THETA ** (-jnp.arange(half, dtype=F32) / half)
    ang = pos.astype(F32)[:, None] * inv[None, :]
    c, s = jnp.cos(ang), jnp.sin(ang)
    return jnp.tile(jnp.concatenate([c, c], axis=1), (1, 2)), jnp.tile(jnp.concatenate([-s, s], axis=1), (1, 2))


def _head_perm():
    c = np.arange(Q_COLS)
    r, g, d = c // KV_W, (c % KV_W) // HEAD_DIM, c % HEAD_DIM
    return (g * GROUP_SIZE + r) * HEAD_DIM + d


def _cover_t(n_cmp, n_cmp_pad):
    cs = np.arange(n_cmp_pad) * CMP_STRIDE
    ss = np.arange(MAX_SEL_BLOCKS) * SEL_BLOCK
    m = (cs[None, :] < ss[:, None] + SEL_BLOCK) & (cs[None, :] + CMP_BLOCK > ss[:, None])
    m = m & (np.arange(n_cmp_pad)[None, :] < n_cmp)
    return jnp.asarray(m, dtype=BF16)


def _block_onehot_t(n_keys):
    k = np.arange(n_keys)
    return jnp.asarray((k[:, None] // SEL_BLOCK) == np.arange(128)[None, :], dtype=BF16)


def _gate_expand():
    ge = np.zeros((N_BRANCH, GATE_PAD, Q_COLS), np.float32)
    c = np.arange(Q_COLS)
    r, g = c // KV_W, (c % KV_W) // HEAD_DIM
    for br in range(N_BRANCH):
        ge[br, (g * GROUP_SIZE + r) * N_BRANCH + br, c] = 1.0
    return jnp.asarray(ge, dtype=BF16)


def _compress_weights(pe, w1, w2):
    half = CMP_BLOCK // 2 * HEAD_DIM
    w1cat = jnp.concatenate([w1[:half], w1[half:]], axis=1).astype(BF16)
    pe2 = jnp.pad(pe.reshape(2, half), ((0, 6), (0, 0)))
    w2bd = jnp.kron(jnp.eye(KV_GROUPS, dtype=F32), w2).astype(BF16)
    return w1cat, pe2, w2bd


def _row(v):
    return v.reshape(1, -1)


TM_PROMPT = 512
TM_SAMPLE = 512


def kernel(x_prompt, x_sample, state_conv, cache_k_cmp, cache_v_cmp, cache_k_sel, cache_v_sel, state_k_win, state_v_win, page_table, p_prompt, p_sample, norm_mix_pre, norm_mix_post, norm_ffn_pre, norm_ffn_post, conv_w_in, conv_b_in, conv_w_dw, conv_b_dw, conv_ln_g, conv_ln_b, conv_w_out, conv_b_out, nsa_w_in, nsa_b_gate, nsa_pe_k, nsa_pe_v, nsa_wk_c1, nsa_wk_c2, nsa_wv_c1, nsa_wv_c2, nsa_w_o, ffn_w_in, ffn_w_out, ple_w_gate, ple_w_proj):
    n_p, seq, _ = x_prompt.shape
    n_s, n_new, _ = x_sample.shape
    n_pages = page_table.shape[1]
    page = cache_k_cmp.shape[2]
    past = n_pages * page
    tp, ts = n_p * seq, n_s * n_new
    assert seq % KEY_CHUNK == 0 and seq // SEL_BLOCK <= MAX_SEL_BLOCKS and seq >= WINDOW + TQ_P
    assert (past + n_new + SEL_BLOCK - 1) // SEL_BLOCK <= MAX_SEL_BLOCKS and n_new <= TQ_S
    assert past % SEL_BLOCK == 0 and page % CMP_STRIDE == 0 and state_k_win.shape[2] == WINDOW

    hp = x_prompt.reshape(tp, D_MODEL)
    hs = x_sample.reshape(ts, D_MODEL)
    pp = p_prompt.reshape(DEPTH, tp, -1)
    ps = p_sample.reshape(DEPTH, ts, -1)

    perm = _head_perm()
    gexp = _gate_expand()
    cos_p, sin_p = _rope_tables(jnp.arange(seq))
    cos_s, sin_s = _rope_tables(past + jnp.arange(ts) % n_new)
    n_cmp_p = seq // CMP_STRIDE - 1
    cov_p = _cover_t(n_cmp_p, seq // CMP_STRIDE)
    n_cmp_s = past // CMP_STRIDE
    cov_s = _cover_t(n_cmp_s, n_cmp_s)
    et_p = _block_onehot_t(seq)
    et_s = _block_onehot_t(past + NEW_PAD)
    eye = jnp.eye(KV_W, dtype=BF16)
    tiles_per_seq = seq // TM_PROMPT

    def pad_rows(x, n):
        return jnp.pad(x, ((0, 0), (0, n - x.shape[1]), (0, 0)))

    def token_minor(c):
        return jnp.transpose(c, (0, 1, 3, 4, 2)).reshape(c.shape[0] * c.shape[1], KV_W, c.shape[2])

    def token_major(t, lead):
        t = t.reshape(lead + (KV_GROUPS, HEAD_DIM, t.shape[-1]))
        return jnp.moveaxis(t, -1, len(lead))

    pools = [token_minor(c) for c in (cache_k_cmp, cache_v_cmp, cache_k_sel, cache_v_sel)]
    n_pool = cache_k_cmp.shape[1]
    kwin_t, vwin_t = token_minor(state_k_win), token_minor(state_v_win)
    conv_state_t = jnp.transpose(state_conv, (0, 2, 1, 3)).reshape(-1, n_s, D_MODEL)

    conv_p, conv_s, nsa_p, nsa_s = [], [], [], []
    for i in range(DEPTH):
        j = i // N_MIXERS
        g_pre, g_post = _row(norm_mix_pre[i]), _row(norm_mix_post[i])
        if i % N_MIXERS == 0:
            w_in = conv_w_in[j].astype(BF16)
            b_in = _row(conv_b_in[j])
            w_out = conv_w_out[j].astype(BF16)
            dw = (conv_w_dw[j], _row(conv_b_dw[j]), _row(conv_ln_g[j]), _row(conv_ln_b[j]))
            glu_p = conv_in(hp, g_pre, w_in, b_in, TM_PROMPT)
            glu_s = conv_in(hs, g_pre, w_in, b_in, TM_SAMPLE)
            z_p = dwconv_prompt(glu_p, n_p, seq, *dw)
            z_s, st_s = dwconv_sample(conv_state_t, j, glu_s.reshape(n_s, n_new, D_MODEL), *dw)
            conv_p.append(glu_p.reshape(n_p, seq, D_MODEL)[:, seq - (CONV_WIDTH - 1):])
            conv_s.append(st_s)
            b_out = _row(conv_b_out[j])
            hp = out_proj_res(z_p, w_out, b_out, g_post, hp, TM_PROMPT)
            hs = out_proj_res(z_s.reshape(ts, D_MODEL), w_out, b_out, g_post, hs, TM_SAMPLE)
        else:
            w = nsa_w_in[j]
            w_in = jnp.concatenate([w[:, :Q_COLS][:, perm], w[:, Q_COLS:],
                                    jnp.zeros((D_MODEL, GATE_PAD - GATE_COLS), F32)], axis=1).astype(BF16)
            bg = jnp.pad(nsa_b_gate[j], (0, GATE_PAD - GATE_COLS)).reshape(1, GATE_PAD)
            w_o = nsa_w_o[j][perm].astype(BF16)
            wk = _compress_weights(nsa_pe_k[j], nsa_wk_c1[j], nsa_wk_c2[j])
            wv = _compress_weights(nsa_pe_v[j], nsa_wv_c1[j], nsa_wv_c2[j])
            zero_b = jnp.zeros((1, D_MODEL), F32)

            qq, kc, vc, kc_t, vc_t, ks_t, vs_t, kw_t, vw_t, kvs, kvw, gate = nsa_in(
                hp, g_pre, w_in, bg, cos_p, sin_p, lambda t: t % tiles_per_seq, TM_PROMPT, seq=seq)
            zero_w2 = jnp.zeros((CMP_HIDDEN, HEAD_DIM), F32)
            w2_packed = jnp.block([[nsa_wk_c2[j], zero_w2], [zero_w2, nsa_wv_c2[j]]]).astype(BF16)
            ckv = compress_prompt(kc, vc, n_p, seq, wk, wv, w2_packed)
            o_p = attn_prompt(qq, gate, ckv, kvs, kvw, n_p, seq, cov_p, et_p, gexp)
            hp = out_proj_res(o_p, w_o, zero_b, g_post, hp, TM_PROMPT)
            keep = min(WINDOW, seq)
            nsa_p.append((kc_t, vc_t, ks_t, vs_t, kw_t[:, :, seq - keep:], vw_t[:, :, seq - keep:]))

            qq, kc, vc, ks, vs, kw, vw, kvb, gate = nsa_in(
                hs, g_pre, w_in, bg, cos_s, sin_s, lambda t: t, TM_SAMPLE)
            r3 = lambda x: x.reshape(n_s, n_new, x.shape[-1])
            ck, cv = compress_sample(pools[0], pools[1], j * n_pool, page_table,
                                     pad_rows(r3(kc), CMP_STRIDE), pad_rows(r3(vc), CMP_STRIDE), wk, wv)
            o_s, kwo, vwo = attn_sample(
                pools[2], pools[3], j * n_pool, page_table,
                pad_rows(r3(qq), TQ_S), pad_rows(r3(gate), TQ_S), ck, cv,
                pad_rows(r3(kvb), TQ_S), pad_rows(r3(kw), TQ_S), pad_rows(r3(vw), TQ_S),
                kwin_t, vwin_t, j * n_s, n_new, cov_s, et_s, gexp, eye)
            hs = out_proj_res(o_s[:, :n_new].reshape(ts, Q_COLS), w_o, zero_b, g_post, hs, TM_SAMPLE)
            kvshape = (n_s, n_new, KV_GROUPS, HEAD_DIM)
            nsa_s.append((kc.reshape(kvshape), vc.reshape(kvshape), ks.reshape(kvshape), vs.reshape(kvshape),
                          kwo, vwo))

        fw = (_row(norm_ffn_pre[i]), _row(norm_ffn_post[i]), ffn_w_in[i].astype(BF16), ffn_w_out[i].astype(BF16),
              ple_w_gate[i].astype(BF16), ple_w_proj[i].astype(BF16))
        hp = ffn_ple(hp, pp[i], *fw, TM_PROMPT)
        hs = ffn_ple(hs, ps[i], *fw, TM_SAMPLE)

    stack = lambda lst, k: jnp.stack([st[k] for st in lst])
    stack_t = lambda lst, k, nb: token_major(stack(lst, k), (len(lst), nb))
    conv_state_s = jnp.transpose(jnp.stack(conv_s), (0, 2, 1, 3))
    return (hp.reshape(x_prompt.shape), hs.reshape(x_sample.shape), jnp.stack(conv_p),
            stack_t(nsa_p, 0, n_p), stack_t(nsa_p, 1, n_p), stack_t(nsa_p, 2, n_p), stack_t(nsa_p, 3, n_p),
            stack_t(nsa_p, 4, n_p), stack_t(nsa_p, 5, n_p),
            conv_state_s,
            stack(nsa_s, 0), stack(nsa_s, 1), stack(nsa_s, 2), stack(nsa_s, 3),
            stack_t(nsa_s, 4, n_s), stack_t(nsa_s, 5, n_s))
```

```python
import functools

import numpy as np
import jax
import jax.numpy as jnp
from jax import lax
from jax.experimental import pallas as pl
from jax.experimental.pallas import tpu as pltpu

F32 = jnp.float32
BF16 = jnp.bfloat16

D_MODEL = 1024
DEPTH = 4
N_MIXERS = 2
CONV_WIDTH = 31
N_HEADS = 16
HEAD_DIM = 64
KV_GROUPS = 4
GROUP_SIZE = 4
N_BRANCH = 3
CMP_BLOCK = 32
CMP_STRIDE = 16
CMP_HIDDEN = 2 * HEAD_DIM
SEL_BLOCK = 64
N_SELECT = 16
WINDOW = 512
D_FF = 4 * D_MODEL
ROPE_THETA = 10000.0
EPS = 1e-6
KV_W = KV_GROUPS * HEAD_DIM
Q_COLS = N_HEADS * HEAD_DIM
KV_COLS = 2 * N_BRANCH * KV_W
GATE_COLS = N_BRANCH * N_HEADS
GATE_PAD = 128
NEG = -1e30
LOG2E = 1.4426950408889634
MAX_SEL_BLOCKS = 64
VMEM_LIMIT = 56 * 1024 * 1024


def _cparams(*sem):
    return pltpu.CompilerParams(dimension_semantics=sem, vmem_limit_bytes=VMEM_LIMIT)


def _rms(x, g):
    return x * lax.rsqrt(jnp.mean(x * x, axis=-1, keepdims=True) + EPS) * g


def _dot(a, b):
    return jnp.dot(a, b, preferred_element_type=F32)


def _dot_t(a, b):
    return lax.dot_general(a, b, (((1,), (1,)), ((), ())), preferred_element_type=F32)


def _split_bf16(x):
    hi = x.astype(BF16)
    lo = (x - hi.astype(F32)).astype(BF16)
    return hi, lo


COL_TILE = 256


def _conv_in_body(x_ref, g_ref, w_ref, b_ref, o_ref):
    u = _rms(x_ref[...], g_ref[...]).astype(BF16)
    for j in range(D_MODEL // COL_TILE):
        lo, hi = j * COL_TILE, (j + 1) * COL_TILE
        a1 = _dot(u, w_ref[:, lo:hi]) + b_ref[:, lo:hi]
        a2 = _dot(u, w_ref[:, D_MODEL + lo:D_MODEL + hi]) + b_ref[:, D_MODEL + lo:D_MODEL + hi]
        o_ref[:, lo:hi] = a1 * jax.nn.sigmoid(a2)


def conv_in(h, g, w, b, tm):
    m = h.shape[0]
    return pl.pallas_call(
        _conv_in_body,
        out_shape=jax.ShapeDtypeStruct((m, D_MODEL), F32),
        grid=(m // tm,),
        in_specs=[
            pl.BlockSpec((tm, D_MODEL), lambda i: (i, 0)),
            pl.BlockSpec((1, D_MODEL), lambda i: (0, 0)),
            pl.BlockSpec((D_MODEL, 2 * D_MODEL), lambda i: (0, 0)),
            pl.BlockSpec((1, 2 * D_MODEL), lambda i: (0, 0)),
        ],
        out_specs=pl.BlockSpec((tm, D_MODEL), lambda i: (i, 0)),
        compiler_params=_cparams("parallel"),
        name="conv_in",
    )(h, g, w, b)


HALO = 32
DW_ROWS = 64
DW_LANES = 256


def _ln_silu(y, ln_g, ln_b):
    mu = jnp.mean(y, axis=-1, keepdims=True)
    yc = y - mu
    var = jnp.mean(yc * yc, axis=-1, keepdims=True)
    yn = yc * lax.rsqrt(var + EPS) * ln_g + ln_b
    return yn * jax.nn.sigmoid(yn)


def _dwconv_prompt_body(ts, cur_ref, halo_ref, w_ref, bdw_ref, lng_ref, lnb_ref, z_ref, xin_ref, y_ref):
    first = pl.program_id(1) == 0

    @pl.when(first)
    def _():
        xin_ref[0:HALO, :] = jnp.zeros((HALO, D_MODEL), F32)

    @pl.when(jnp.logical_not(first))
    def _():
        xin_ref[0:HALO, :] = halo_ref[...]

    xin_ref[HALO:HALO + ts, :] = cur_ref[...]
    xin_ref[HALO + ts:HALO + ts + 8, :] = jnp.zeros((8, D_MODEL), F32)
    off = HALO - (CONV_WIDTH - 1)
    for rc in range(ts // DW_ROWS):
        for lc in range(D_MODEL // DW_LANES):
            l0, l1 = lc * DW_LANES, (lc + 1) * DW_LANES
            acc = jnp.zeros((DW_ROWS, DW_LANES), F32)
            for b in range(8):
                z = None
                for a in range((off + CONV_WIDTH + 7) // 8):
                    k = 8 * a + b - off
                    if 0 <= k < CONV_WIDTH:
                        r0 = rc * DW_ROWS + 8 * a
                        term = xin_ref[r0:r0 + DW_ROWS + 8, l0:l1] * w_ref[k:k + 1, l0:l1]
                        z = term if z is None else z + term
                acc = acc + z[b:b + DW_ROWS]
            y_ref[rc * DW_ROWS:(rc + 1) * DW_ROWS, l0:l1] = acc
    y = y_ref[...] + bdw_ref[...]
    z_ref[...] = _ln_silu(y, lng_ref[...], lnb_ref[...]).astype(BF16)


def dwconv_prompt(glu, n_seq, seq, w_dw, b_dw, ln_g, ln_b, ts=256):
    nt = seq // ts
    hb = ts // HALO
    return pl.pallas_call(
        functools.partial(_dwconv_prompt_body, ts),
        out_shape=jax.ShapeDtypeStruct((n_seq * seq, D_MODEL), BF16),
        grid=(n_seq, nt),
        in_specs=[
            pl.BlockSpec((ts, D_MODEL), lambda b, i: (b * nt + i, 0)),
            pl.BlockSpec((HALO, D_MODEL), lambda b, i: (jnp.maximum((b * nt + i) * hb - 1, 0), 0)),
            pl.BlockSpec((CONV_WIDTH, D_MODEL), lambda b, i: (0, 0)),
            pl.BlockSpec((1, D_MODEL), lambda b, i: (0, 0)),
            pl.BlockSpec((1, D_MODEL), lambda b, i: (0, 0)),
            pl.BlockSpec((1, D_MODEL), lambda b, i: (0, 0)),
        ],
        out_specs=pl.BlockSpec((ts, D_MODEL), lambda b, i: (b * nt + i, 0)),
        scratch_shapes=[pltpu.VMEM((HALO + ts + 8, D_MODEL), F32), pltpu.VMEM((ts, D_MODEL), F32)],
        compiler_params=_cparams("parallel", "parallel"),
        name="dwconv_prompt",
    )(glu, glu, w_dw, b_dw, ln_g, ln_b)


def _dwconv_sample_body(n_new, st_ref, glu_ref, w_ref, bdw_ref, lng_ref, lnb_ref, z_ref, ns_ref):
    n_st = CONV_WIDTH - 1
    rows = [st_ref[j] for j in range(n_st)] + [glu_ref[:, j, :] for j in range(n_new)]
    for t in range(n_new):
        acc = rows[t] * w_ref[0:1, :]
        for k in range(1, CONV_WIDTH):
            acc = acc + rows[t + k] * w_ref[k:k + 1, :]
        y = acc + bdw_ref[...]
        z_ref[:, t, :] = _ln_silu(y, lng_ref[...], lnb_ref[...])
    for j in range(n_st):
        ns_ref[j] = rows[j + n_new]


def dwconv_sample(state, layer, glu, w_dw, b_dw, ln_g, ln_b, bb=8):
    nb, n_new, _ = glu.shape
    n_st = CONV_WIDTH - 1
    vec = pl.BlockSpec((1, D_MODEL), lambda i: (0, 0))
    return pl.pallas_call(
        functools.partial(_dwconv_sample_body, n_new),
        out_shape=(jax.ShapeDtypeStruct((nb, n_new, D_MODEL), F32),
                   jax.ShapeDtypeStruct((n_st, nb, D_MODEL), F32)),
        grid=(nb // bb,),
        in_specs=[
            pl.BlockSpec((n_st, bb, D_MODEL), lambda i: (layer, i, 0)),
            pl.BlockSpec((bb, n_new, D_MODEL), lambda i: (i, 0, 0)),
            pl.BlockSpec((CONV_WIDTH, D_MODEL), lambda i: (0, 0)),
            vec, vec, vec,
        ],
        out_specs=(pl.BlockSpec((bb, n_new, D_MODEL), lambda i: (i, 0, 0)),
                   pl.BlockSpec((n_st, bb, D_MODEL), lambda i: (0, i, 0))),
        compiler_params=_cparams("parallel"),
        name="dwconv_sample",
    )(state, glu, w_dw, b_dw, ln_g, ln_b)


def _out_proj_body(z_ref, w_ref, b_ref, g_ref, h_ref, o_ref):
    m = _dot(z_ref[...].astype(BF16), w_ref[...]) + b_ref[...]
    o_ref[...] = h_ref[...] + _rms(m, g_ref[...])


def out_proj_res(z, w, b, g, h, tm):
    m = h.shape[0]
    vec = pl.BlockSpec((1, D_MODEL), lambda i: (0, 0))
    row = pl.BlockSpec((tm, D_MODEL), lambda i: (i, 0))
    return pl.pallas_call(
        _out_proj_body,
        out_shape=jax.ShapeDtypeStruct((m, D_MODEL), F32),
        grid=(m // tm,),
        in_specs=[row, pl.BlockSpec((D_MODEL, D_MODEL), lambda i: (0, 0)), vec, vec, row],
        out_specs=row,
        compiler_params=_cparams("parallel"),
        name="out_proj_res",
    )(z, w, b, g, h)


FF_CHUNK = 1024


def _ffn_body(h_ref, p_ref, gpre_ref, gpost_ref, win_ref, wout_ref, wpg_ref, wpp_ref, o_ref, u_ref, acc_ref):
    j = pl.program_id(1)

    @pl.when(j == 0)
    def _():
        u_ref[...] = _rms(h_ref[...], gpre_ref[...]).astype(BF16)
        acc_ref[...] = jnp.zeros_like(acc_ref)

    a = _dot(u_ref[...], win_ref[...])
    a = jnp.square(jnp.maximum(a, 0.0)).astype(BF16)
    acc_ref[...] += _dot(a, wout_ref[...])

    @pl.when(j == pl.num_programs(1) - 1)
    def _():
        h2 = h_ref[...] + _rms(acc_ref[...], gpost_ref[...])
        gate = jax.nn.sigmoid(_dot(h2.astype(BF16), wpg_ref[...]))
        proj = _dot(p_ref[...].astype(BF16), wpp_ref[...])
        o_ref[...] = h2 + gate * proj


def ffn_ple(h, p, g_pre, g_post, w_in, w_out, w_pg, w_pp, tm):
    m = h.shape[0]
    ple = p.shape[1]
    vec = pl.BlockSpec((1, D_MODEL), lambda i, j: (0, 0))
    row = pl.BlockSpec((tm, D_MODEL), lambda i, j: (i, 0))
    return pl.pallas_call(
        _ffn_body,
        out_shape=jax.ShapeDtypeStruct((m, D_MODEL), F32),
        grid=(m // tm, D_FF // FF_CHUNK),
        in_specs=[
            row,
            pl.BlockSpec((tm, ple), lambda i, j: (i, 0)),
            vec, vec,
            pl.BlockSpec((D_MODEL, FF_CHUNK), lambda i, j: (0, j)),
            pl.BlockSpec((FF_CHUNK, D_MODEL), lambda i, j: (j, 0)),
            pl.BlockSpec((D_MODEL, D_MODEL), lambda i, j: (0, 0)),
            pl.BlockSpec((ple, D_MODEL), lambda i, j: (0, 0)),
        ],
        out_specs=row,
        scratch_shapes=[pltpu.VMEM((tm, D_MODEL), BF16), pltpu.VMEM((tm, D_MODEL), F32)],
        compiler_params=_cparams("parallel", "arbitrary"),
        name="ffn_ple",
    )(h, p, g_pre, g_post, w_in, w_out, w_pg, w_pp)


NSA_W_COLS = Q_COLS + KV_COLS + GATE_PAD


def _pack_groups(k, v, out_ref):
    low = lax.broadcasted_iota(jnp.int32, (k.shape[0], 128), 1) < HEAD_DIM
    for pair in range(KV_GROUPS // 2):
        kp = k[:, pair * 128:(pair + 1) * 128]
        vp = v[:, pair * 128:(pair + 1) * 128]
        out_ref[0, 2 * pair] = jnp.where(low, kp, pltpu.roll(vp, HEAD_DIM, 1)).astype(BF16)
        out_ref[0, 2 * pair + 1] = jnp.where(low, pltpu.roll(kp, HEAD_DIM, 1), vp).astype(BF16)


def _nsa_in_body(transposed, x_ref, g_ref, w_ref, bg_ref, cos_ref, sin_ref, qq_ref, *out_refs):
    if transposed:
        kc_ref, vc_ref = out_refs[0:2]
        t_refs = out_refs[2:8]
        outs = (kc_ref, vc_ref, None, None, None, None)
        packed_refs = out_refs[8:10]
        kvb_ref = None
    else:
        outs = out_refs[0:6]
        t_refs = None
        kvb_ref = out_refs[6]
    gate_ref = out_refs[-1]
    u = _rms(x_ref[...], g_ref[...]).astype(BF16)
    tm = u.shape[0]
    cos = jnp.concatenate([cos_ref[...], cos_ref[...]], axis=1)
    sin = jnp.concatenate([sin_ref[...], sin_ref[...]], axis=1)
    lane = lax.broadcasted_iota(jnp.int32, (tm, KV_W), 1)
    first_half = (lane & (HEAD_DIM - 1)) < (HEAD_DIM // 2)

    def rope(x):
        rot = jnp.where(first_half, pltpu.roll(x, KV_W - HEAD_DIM // 2, 1), pltpu.roll(x, HEAD_DIM // 2, 1))
        return x * cos + rot * sin

    scale = HEAD_DIM ** -0.5 * LOG2E
    for j in range(Q_COLS // KV_W):
        q = _dot(u, w_ref[:, j * KV_W:(j + 1) * KV_W]) * scale
        qq_ref[:, j * KV_W:(j + 1) * KV_W] = q.astype(BF16)
        qq_ref[:, Q_COLS + j * KV_W:Q_COLS + (j + 1) * KV_W] = rope(q).astype(BF16)
    for j in range(2 * N_BRANCH):
        x = _dot(u, w_ref[:, Q_COLS + j * KV_W:Q_COLS + (j + 1) * KV_W])
        if j in (2, 4):
            x = rope(x)
        if outs[j] is not None:
            outs[j][...] = x
        if transposed:
            t_refs[j][0] = jnp.transpose(x)
            if j in (2, 4):
                k_rows = x
            if j in (3, 5):
                _pack_groups(k_rows, x, packed_refs[(j - 3) // 2])
        elif j >= 2:
            kvb_ref[:, (j - 2) * KV_W:(j - 1) * KV_W] = x.astype(BF16)
    gl = _dot(u, w_ref[:, Q_COLS + KV_COLS:]) + bg_ref[...]
    gate_ref[...] = jax.nn.sigmoid(gl)


def nsa_in(h, g, w, bg, cos_t, sin_t, pos_map, tm, seq=None):
    m = h.shape[0]
    kv_spec = pl.BlockSpec((tm, KV_W), lambda i: (i, 0))
    kv_shape = jax.ShapeDtypeStruct((m, KV_W), F32)
    if seq is None:
        kv_shapes = (kv_shape,) * 6 + (jax.ShapeDtypeStruct((m, 4 * KV_W), BF16),)
        kv_specs = (kv_spec,) * 6 + (pl.BlockSpec((tm, 4 * KV_W), lambda i: (i, 0)),)
    else:
        nt = seq // tm
        t_shape = jax.ShapeDtypeStruct((m // seq, KV_W, seq), F32)
        t_spec = pl.BlockSpec((1, KV_W, tm), lambda i: (i // nt, 0, i % nt))
        p_shape = jax.ShapeDtypeStruct((m // seq, KV_GROUPS, seq, 128), BF16)
        p_spec = pl.BlockSpec((1, KV_GROUPS, tm, 128), lambda i: (i // nt, 0, i % nt, 0))
        kv_shapes = (kv_shape,) * 2 + (t_shape,) * 6 + (p_shape,) * 2
        kv_specs = (kv_spec,) * 2 + (t_spec,) * 6 + (p_spec,) * 2
    return pl.pallas_call(
        functools.partial(_nsa_in_body, seq is not None),
        out_shape=(jax.ShapeDtypeStruct((m, 2 * Q_COLS), BF16),) + kv_shapes + (
                   jax.ShapeDtypeStruct((m, GATE_PAD), F32),),
        grid=(m // tm,),
        in_specs=[
            pl.BlockSpec((tm, D_MODEL), lambda i: (i, 0)),
            pl.BlockSpec((1, D_MODEL), lambda i: (0, 0)),
            pl.BlockSpec((D_MODEL, NSA_W_COLS), lambda i: (0, 0)),
            pl.BlockSpec((1, GATE_PAD), lambda i: (0, 0)),
            pl.BlockSpec((tm, 2 * HEAD_DIM), lambda i: (pos_map(i), 0)),
            pl.BlockSpec((tm, 2 * HEAD_DIM), lambda i: (pos_map(i), 0)),
        ],
        out_specs=(pl.BlockSpec((tm, 2 * Q_COLS), lambda i: (i, 0)),) + kv_specs + (
                   pl.BlockSpec((tm, GATE_PAD), lambda i: (i, 0)),),
        compiler_params=_cparams("parallel"),
        name="nsa_in",
    )(h, g, w, bg, cos_t, sin_t)


CHUNK_W = CMP_STRIDE * HEAD_DIM
TOK_PAIRS = CMP_STRIDE // 2


def _compress_one(load_t, nc, n_out, w1_ref, pe_ref, w2_ref, out_ref, c_ref, ab_ref):
    lane = lax.broadcasted_iota(jnp.int32, (nc, 2 * HEAD_DIM), 1)
    low = lane < HEAD_DIM
    for tp in range(TOK_PAIRS):
        for half in range(KV_GROUPS // 2):
            a = load_t(2 * tp, half)
            b = load_t(2 * tp + 1, half)
            even = jnp.where(low, a, pltpu.roll(b, HEAD_DIM, 1))
            odd = jnp.where(low, pltpu.roll(a, HEAD_DIM, 1), b)
            g0 = 2 * half
            c_ref[g0 * nc:(g0 + 1) * nc, tp * 128:(tp + 1) * 128] = even.astype(BF16)
            c_ref[(g0 + 1) * nc:(g0 + 2) * nc, tp * 128:(tp + 1) * 128] = odd.astype(BF16)
    ab_ref[0:KV_GROUPS * nc, :] = _dot(c_ref[...], w1_ref[...])
    ab_ref[KV_GROUPS * nc:KV_GROUPS * nc + 8, :] = jnp.zeros((8, 2 * CMP_HIDDEN), F32)
    pe2 = _dot(pe_ref[...].astype(BF16), w1_ref[...])
    pe_c = pe2[0:1, 0:CMP_HIDDEN] + pe2[1:2, CMP_HIDDEN:2 * CMP_HIDDEN]
    hs = []
    for g in range(KV_GROUPS):
        top = ab_ref[g * nc:g * nc + n_out, 0:CMP_HIDDEN]
        bot = ab_ref[g * nc + 1:g * nc + 1 + n_out, CMP_HIDDEN:2 * CMP_HIDDEN]
        hs.append(jax.nn.gelu(top + bot + pe_c).astype(BF16))
    if out_ref is None:
        return hs
    hcat = jnp.concatenate(hs, axis=1)
    out_ref[0] = _dot(hcat, w2_ref[...]).astype(BF16)


def _compress_prompt_body(nc, k0_ref, k1_ref, v0_ref, v1_ref, wk1_ref, pek_ref, wv1_ref, pev_ref, w2_ref,
                          ckv_ref, c_ref, ab_ref):
    def loader(halves):
        return lambda t, half: halves[half][pl.ds(t, nc, stride=CMP_STRIDE), :]

    hk = _compress_one(loader((k0_ref, k1_ref)), nc, nc, wk1_ref, pek_ref, None, None, c_ref, ab_ref)
    hv = _compress_one(loader((v0_ref, v1_ref)), nc, nc, wv1_ref, pev_ref, None, None, c_ref, ab_ref)
    for g in range(KV_GROUPS):
        ckv_ref[0, g] = _dot(jnp.concatenate([hk[g], hv[g]], axis=1), w2_ref[...]).astype(BF16)


def _compress_weight_specs(idx):
    return [pl.BlockSpec((CHUNK_W, 2 * CMP_HIDDEN), idx),
            pl.BlockSpec((8, CHUNK_W), idx),
            pl.BlockSpec((KV_GROUPS * CMP_HIDDEN, KV_W), idx)]


def compress_prompt(kc, vc, n_seq, seq, wk, wv, w2_packed):
    nc = seq // CMP_STRIDE
    idx = lambda b: (0, 0)
    halves = [pl.BlockSpec((seq, 128), lambda b: (b, 0)), pl.BlockSpec((seq, 128), lambda b: (b, 1))]
    w1_specs = _compress_weight_specs(idx)[0:2]
    return pl.pallas_call(
        functools.partial(_compress_prompt_body, nc),
        out_shape=jax.ShapeDtypeStruct((n_seq, KV_GROUPS, nc, 128), BF16),
        grid=(n_seq,),
        in_specs=halves + halves + w1_specs + w1_specs + [pl.BlockSpec(w2_packed.shape, idx)],
        out_specs=pl.BlockSpec((1, KV_GROUPS, nc, 128), lambda b: (b, 0, 0, 0)),
        scratch_shapes=[pltpu.VMEM((KV_GROUPS * nc, CHUNK_W), BF16),
                        pltpu.VMEM((KV_GROUPS * nc + 8, 2 * CMP_HIDDEN), F32)],
        compiler_params=_cparams("parallel"),
        name="compress_prompt",
    )(kc, kc, vc, vc, wk[0], wk[1], wv[0], wv[1], w2_packed)


def _compress_sample_body(n_pages, page, nc, n_out, pt_ref, *refs):
    k_pages = refs[0:n_pages]
    v_pages = refs[n_pages:2 * n_pages]
    k_new, v_new = refs[2 * n_pages:2 * n_pages + 2]
    wk1_ref, pek_ref, wk2_ref, wv1_ref, pev_ref, wv2_ref = refs[2 * n_pages + 2:2 * n_pages + 8]
    ck_ref, cv_ref, c_ref, ab_ref, x0_ref, x1_ref = refs[2 * n_pages + 8:]
    past = n_pages * page
    n_rows = nc * CMP_STRIDE
    halves = (x0_ref, x1_ref)

    def run(pages, new_ref, w1_ref, pe_ref, w2_ref, out_ref):
        for p, page_ref in enumerate(pages):
            for half in range(2):
                halves[half][p * page:(p + 1) * page, :] = jnp.transpose(page_ref[0, half * 128:(half + 1) * 128, :])
        for half in range(2):
            halves[half][past:past + CMP_STRIDE, :] = new_ref[0, :, half * 128:(half + 1) * 128]
            halves[half][past + CMP_STRIDE:n_rows, :] = jnp.zeros((n_rows - past - CMP_STRIDE, 128), F32)
        load_t = lambda t, half: halves[half][pl.ds(t, nc, stride=CMP_STRIDE), :]
        _compress_one(load_t, nc, n_out, w1_ref, pe_ref, w2_ref, out_ref, c_ref, ab_ref)

    run(k_pages, k_new, wk1_ref, pek_ref, wk2_ref, ck_ref)
    run(v_pages, v_new, wv1_ref, pev_ref, wv2_ref, cv_ref)


def compress_sample(pool_k, pool_v, page0, page_table, k_new, v_new, wk, wv):
    nb, n_pages = page_table.shape
    page = pool_k.shape[2]
    assert page == 128
    n_chunk = n_pages * (page // CMP_STRIDE) + 1
    n_out = n_chunk - 1
    nc = -(-(n_chunk + 1) // 16) * 16
    idx = lambda b, pt: (0, 0)
    page_specs = [pl.BlockSpec((1, KV_W, page), functools.partial(lambda p, b, pt: (page0 + pt[b * n_pages + p], 0, 0), p))
                  for p in range(n_pages)]
    new_spec = pl.BlockSpec((1, CMP_STRIDE, KV_W), lambda b, pt: (b, 0, 0))
    out = jax.ShapeDtypeStruct((nb, n_out, KV_W), BF16)
    grid_spec = pltpu.PrefetchScalarGridSpec(
        num_scalar_prefetch=1,
        grid=(nb,),
        in_specs=page_specs * 2 + [new_spec, new_spec]
        + _compress_weight_specs(idx) + _compress_weight_specs(idx),
        out_specs=(pl.BlockSpec((1, n_out, KV_W), lambda b, pt: (b, 0, 0)),
                   pl.BlockSpec((1, n_out, KV_W), lambda b, pt: (b, 0, 0))),
        scratch_shapes=[pltpu.VMEM((KV_GROUPS * nc, CHUNK_W), BF16),
                        pltpu.VMEM((KV_GROUPS * nc + 8, 2 * CMP_HIDDEN), F32),
                        pltpu.VMEM((nc * CMP_STRIDE, 128), F32),
                        pltpu.VMEM((nc * CMP_STRIDE, 128), F32)],
    )
    return pl.pallas_call(
        functools.partial(_compress_sample_body, n_pages, page, nc, n_out),
        out_shape=(out, out),
        grid_spec=grid_spec,
        compiler_params=_cparams("parallel"),
        name="compress_sample",
    )(page_table.reshape(-1), *([pool_k] * n_pages), *([pool_v] * n_pages), k_new, v_new, *wk, *wv)


def _group_masks():
    lane = lax.broadcasted_iota(jnp.int32, (1, KV_W), 1)
    return [(lane // HEAD_DIM) == g for g in range(KV_GROUPS)]


def _stack_q(q, gmask):
    if q.shape[0] % 16:
        q = q.astype(F32)
    zero = jnp.zeros((q.shape[0], KV_W), q.dtype)
    return jnp.concatenate([jnp.where(gmask[g], q[:, r * KV_W:(r + 1) * KV_W], zero)
                            for g in range(KV_GROUPS) for r in range(GROUP_SIZE)], axis=0).astype(BF16)


def _unstack_o(o, tq, gmask):
    cols = []
    for r in range(GROUP_SIZE):
        acc = jnp.zeros((tq, KV_W), F32)
        for g in range(KV_GROUPS):
            row = (g * GROUP_SIZE + r) * tq
            acc = acc + jnp.where(gmask[g], o[row:row + tq, :], 0.0)
        cols.append(acc)
    return jnp.concatenate(cols, axis=1)


def _rep_rows(x, tq):
    return jnp.concatenate([x[g * tq:(g + 1) * tq] for g in range(KV_GROUPS) for _ in range(GROUP_SIZE)], axis=0)


def _softmax_rows(s, mask):
    s = jnp.where(mask, s, NEG)
    m = jnp.max(s, axis=1, keepdims=True)
    e = jnp.where(mask, jnp.exp2(s - m), 0.0)
    d = jnp.sum(e, axis=1, keepdims=True)
    return e * (1.0 / jnp.where(d > 0, d, 1.0))


def _cmp_branch(qc, ck, cv, qpos0, tq):
    rows, n = qc.shape[0], ck.shape[0]
    s = _dot_t(qc, ck)
    col = lax.broadcasted_iota(jnp.int32, (rows, n), 1)
    qp = qpos0 + (lax.broadcasted_iota(jnp.int32, (rows, n), 0) & (tq - 1))
    p = _softmax_rows(s, col * CMP_STRIDE + (CMP_BLOCK - 1) <= qp)
    o = _dot(p.astype(BF16), cv)
    psum = []
    for g in range(KV_GROUPS):
        acc = p[g * GROUP_SIZE * tq:(g * GROUP_SIZE + 1) * tq]
        for r in range(1, GROUP_SIZE):
            acc = acc + p[(g * GROUP_SIZE + r) * tq:(g * GROUP_SIZE + r + 1) * tq]
        psum.append(acc)
    return o, jnp.concatenate(psum, axis=0)


def _select_bias(psum, cov_t, qpos0, tq, n_live, score_ref):
    hi, lo = _split_bf16(psum)
    imp = _dot_t(cov_t, hi) + _dot_t(cov_t, lo)
    shape = imp.shape
    blk = lax.broadcasted_iota(jnp.int32, shape, 0)
    cur = (qpos0 + (lax.broadcasted_iota(jnp.int32, shape, 1) & (tq - 1))) // SEL_BLOCK
    forced = (blk == 0) | (blk == cur) | (blk == cur - 1)
    score = jnp.where(blk > cur, -jnp.inf, jnp.where(forced, jnp.inf, imp))
    score_ref[...] = score

    def count(k, rank):
        sk = jnp.broadcast_to(score_ref[pl.ds(k, 1), :], shape)
        ge = jnp.where(sk >= score, 1.0, 0.0)
        gt = jnp.where(sk > score, 1.0, 0.0)
        return rank + jnp.where(blk > k, ge, gt)

    rank = lax.fori_loop(0, n_live, count, jnp.zeros(shape, F32))
    bias_t = jnp.where(rank < N_SELECT, 0.0, NEG)
    bias_t = jnp.concatenate([bias_t, jnp.zeros(shape, F32)], axis=0)
    return jnp.transpose(bias_t).astype(BF16)


def _gate_mix(gate, gexp_ref, outs):
    hi, lo = _split_bf16(gate)
    total = None
    for br, o in enumerate(outs):
        ge = _dot(hi, gexp_ref[br]) + _dot(lo, gexp_ref[br])
        total = ge * o if total is None else total + ge * o
    return total


TQ_P = 256
KEY_CHUNK = 512
SM_ROWS = 32


def _attn_prompt_body(seq, qq_ref, gate_ref, ckv_ref, kvs_ref, kvw_ref, cov_ref, et_ref, gexp_ref,
                      o_ref, score_ref, s_ref, p_ref, b_ref, psum_ref, m_ref, l_ref, acc_ref, alpha_ref):
    tq = TQ_P
    rows = GROUP_SIZE * tq
    n = pl.program_id(1)
    q0 = n * tq
    low = lax.broadcasted_iota(jnp.int32, (tq, 128), 1) < HEAD_DIM

    def q_rows(off, g):
        pieces = []
        for r in range(GROUP_SIZE):
            c0 = off + r * KV_W + (g // 2) * 128
            pair = qq_ref[:, c0:c0 + 128].astype(F32)
            if g % 2:
                pair = pltpu.roll(pair, HEAD_DIM, 1)
            pieces.append(jnp.where(low, pair, 0.0))
        return jnp.concatenate(pieces, axis=0).astype(BF16)

    def blocks(fn):
        for r0 in range(0, rows, SM_ROWS):
            fn(r0, r0 % tq)

    def block_iotas(width):
        return (lax.broadcasted_iota(jnp.int32, (SM_ROWS, width), 0),
                lax.broadcasted_iota(jnp.int32, (SM_ROWS, width), 1))

    def lanes(x, width):
        return jnp.concatenate([x] * (width // 128), axis=1)

    qc = [q_rows(0, g) for g in range(KV_GROUPS)]
    qr = [q_rows(Q_COLS, g) for g in range(KV_GROUPS)]

    def slot(g):
        return s_ref, p_ref, alpha_ref

    n_cmp = ckv_ref.shape[2]
    psum_ref[...] = jnp.zeros_like(psum_ref)
    o_c = []
    for g in range(KV_GROUPS):
        ckv = ckv_ref[0, g]
        s_g, p_g, _ = slot(g)
        s_g[:, 0:n_cmp] = _dot_t(qc[g], ckv)

        def cmp_block(r0, qoff, g=g, s_g=s_g, p_g=p_g):
            row, col = block_iotas(n_cmp)
            p = _softmax_rows(s_g[pl.ds(r0, SM_ROWS), 0:n_cmp],
                              col * CMP_STRIDE + (CMP_BLOCK - 1) <= q0 + qoff + row)
            p_g[pl.ds(r0, SM_ROWS), 0:n_cmp] = p.astype(BF16)
            psum_ref[pl.ds(g * tq + qoff, SM_ROWS), :] += p

        blocks(cmp_block)
        o_c.append(_dot(p_g[:, 0:n_cmp], ckv))
    bias = _select_bias(psum_ref[...], cov_ref[...], q0, tq,
                        (q0 + tq - 1) // SEL_BLOCK + 1, score_ref)

    m_ref[...] = jnp.full(m_ref.shape, NEG, F32)
    l_ref[...] = jnp.zeros_like(l_ref)
    acc_ref[...] = jnp.zeros_like(acc_ref)

    def chunk(c, carry, causal):
        k0 = pl.multiple_of(c * KEY_CHUNK, KEY_CHUNK)
        b_ref[...] = _dot_t(bias, et_ref[pl.ds(k0, KEY_CHUNK), :])
        for g in range(KV_GROUPS):
            kv = kvs_ref[0, g, pl.ds(k0, KEY_CHUNK), :]
            s_g, p_g, a_g = slot(g)
            s_g[:, 0:KEY_CHUNK] = _dot_t(qr[g], kv)

            def sel_block(r0, qoff, g=g, s_g=s_g, p_g=p_g, a_g=a_g):
                t = s_g[pl.ds(r0, SM_ROWS), 0:KEY_CHUNK] + b_ref[pl.ds(g * tq + qoff, SM_ROWS), :]
                if causal:
                    row, col = block_iotas(KEY_CHUNK)
                    t = jnp.where(k0 + col <= q0 + qoff + row, t, NEG)
                m_old = m_ref[g, pl.ds(r0, SM_ROWS), :]
                m_new = jnp.maximum(m_old, jnp.max(t, axis=1, keepdims=True))
                alpha = jnp.exp2(m_old - m_new)
                p = jnp.exp2(t - lanes(m_new, KEY_CHUNK))
                l_ref[g, pl.ds(r0, SM_ROWS), :] = (alpha * l_ref[g, pl.ds(r0, SM_ROWS), :]
                                                   + jnp.sum(p, axis=1, keepdims=True))
                m_ref[g, pl.ds(r0, SM_ROWS), :] = m_new
                a_g[pl.ds(r0, SM_ROWS), :] = alpha
                p_g[pl.ds(r0, SM_ROWS), 0:KEY_CHUNK] = p.astype(BF16)

            blocks(sel_block)
            acc_ref[g] = a_g[...] * acc_ref[g] + _dot(p_g[:, 0:KEY_CHUNK], kv)
        return carry

    diag = q0 // KEY_CHUNK
    lax.fori_loop(0, diag, functools.partial(chunk, causal=False), 0)
    chunk(diag, 0, True)
    o_s = [acc_ref[g] * (1.0 / l_ref[g]) for g in range(KV_GROUPS)]

    nk = WINDOW + tq
    base = pl.multiple_of(jnp.maximum(q0 - WINDOW, 0), tq)
    o_w = []
    for g in range(KV_GROUPS):
        kv = kvw_ref[0, g, pl.ds(base, nk), :]
        s_g, p_g, a_g = slot(g)
        s_g[:, 0:nk] = _dot_t(qr[g], kv)

        def win_block(r0, qoff, s_g=s_g, p_g=p_g, a_g=a_g):
            row, col = block_iotas(nk)
            kpos = base + col
            qp = q0 + qoff + row
            t = s_g[pl.ds(r0, SM_ROWS), 0:nk]
            t = jnp.where(kpos <= qp, jnp.where(kpos > qp - WINDOW, t, NEG), NEG)
            p = jnp.exp2(t - jnp.max(t, axis=1, keepdims=True))
            a_g[pl.ds(r0, SM_ROWS), :] = jnp.broadcast_to(jnp.sum(p, axis=1, keepdims=True), (SM_ROWS, 128))
            p_g[pl.ds(r0, SM_ROWS), 0:nk] = p.astype(BF16)

        blocks(win_block)
        o_w.append(_dot(p_g[:, 0:nk], kv) * (1.0 / a_g[...]))

    def to_heads(o):
        tiles = []
        for r in range(GROUP_SIZE):
            for pair in range(KV_GROUPS // 2):
                even = pltpu.roll(o[2 * pair][r * tq:(r + 1) * tq], HEAD_DIM, 1)
                tiles.append(jnp.where(low, even, o[2 * pair + 1][r * tq:(r + 1) * tq]))
        return jnp.concatenate(tiles, axis=1)

    outs = [to_heads(o) for o in (o_c, o_s, o_w)]
    o_ref[...] = _gate_mix(gate_ref[...], gexp_ref, outs).astype(BF16)


def attn_prompt(qq, gate, ckv, kvs, kvw, n_seq, seq, cov_t, e_t, gexp):
    tq = TQ_P
    nt = seq // tq
    n_cmp = ckv.shape[2]
    rows = GROUP_SIZE * tq
    max_keys = max(WINDOW + tq, KEY_CHUNK, n_cmp)
    return pl.pallas_call(
        functools.partial(_attn_prompt_body, seq),
        out_shape=jax.ShapeDtypeStruct((n_seq * seq, Q_COLS), BF16),
        grid=(n_seq, nt),
        in_specs=[
            pl.BlockSpec((tq, 2 * Q_COLS), lambda b, i: (b * nt + i, 0)),
            pl.BlockSpec((tq, GATE_PAD), lambda b, i: (b * nt + i, 0)),
            pl.BlockSpec((1, KV_GROUPS, n_cmp, 128), lambda b, i: (b, 0, 0, 0)),
            pl.BlockSpec((1, KV_GROUPS, seq, 128), lambda b, i: (b, 0, 0, 0), pipeline_mode=pl.Buffered(1)),
            pl.BlockSpec((1, KV_GROUPS, seq, 128), lambda b, i: (b, 0, 0, 0), pipeline_mode=pl.Buffered(1)),
            pl.BlockSpec(cov_t.shape, lambda b, i: (0, 0)),
            pl.BlockSpec(e_t.shape, lambda b, i: (0, 0), pipeline_mode=pl.Buffered(1)),
            pl.BlockSpec(gexp.shape, lambda b, i: (0, 0, 0)),
        ],
        out_specs=pl.BlockSpec((tq, Q_COLS), lambda b, i: (b * nt + i, 0)),
        scratch_shapes=[
            pltpu.VMEM((MAX_SEL_BLOCKS, KV_GROUPS * tq), F32),
            pltpu.VMEM((rows, max_keys), F32),
            pltpu.VMEM((rows, max_keys), BF16),
            pltpu.VMEM((KV_GROUPS * tq, KEY_CHUNK), F32),
            pltpu.VMEM((KV_GROUPS * tq, n_cmp), F32),
            pltpu.VMEM((KV_GROUPS, rows, 128), F32),
            pltpu.VMEM((KV_GROUPS, rows, 128), F32),
            pltpu.VMEM((KV_GROUPS, rows, 128), F32),
            pltpu.VMEM((rows, 128), F32),
        ],
        compiler_params=_cparams("parallel", "parallel"),
        name="attn_prompt",
    )(qq, gate, ckv, kvs, kvw, cov_t, e_t, gexp)


TQ_S = 8
NEW_PAD = 128


def _attn_sample_body(n_pages, page, n_new, pt_ref, *refs):
    ks_pages = refs[0:n_pages]
    vs_pages = refs[n_pages:2 * n_pages]
    (qq_ref, gate_ref, ck_ref, cv_ref, kvn_ref, kwn_ref, vwn_ref, kwb_ref, vwb_ref,
     cov_ref, et_ref, gexp_ref, eye_ref, o_ref, kwo_ref, vwo_ref, score_ref) = refs[2 * n_pages:]
    tq = TQ_S
    rows = KV_GROUPS * GROUP_SIZE * tq
    past = n_pages * page
    gmask = _group_masks()
    qc = _stack_q(qq_ref[0, :, 0:Q_COLS], gmask)
    qr = _stack_q(qq_ref[0, :, Q_COLS:2 * Q_COLS], gmask)

    o_c, psum = _cmp_branch(qc, ck_ref[0], cv_ref[0], past, tq)
    n_live = (past + tq - 1) // SEL_BLOCK + 1
    bias = _select_bias(psum, cov_ref[...], past, tq, n_live, score_ref)

    pad = jnp.zeros((NEW_PAD - tq, KV_W), BF16)
    eye = eye_ref[...]

    def new_cols(col):
        x = jnp.concatenate([kvn_ref[0, :, col * KV_W:(col + 1) * KV_W], pad], axis=0)
        return _dot_t(eye, x).astype(BF16)

    k_all = jnp.concatenate([p[0].astype(BF16) for p in ks_pages] + [new_cols(0)], axis=1)
    v_all = jnp.concatenate([p[0].astype(BF16) for p in vs_pages] + [new_cols(1)], axis=1)
    nk = past + NEW_PAD
    s = _dot(qr, k_all) + _rep_rows(_dot_t(bias, et_ref[...]), tq)
    kpos = lax.broadcasted_iota(jnp.int32, (rows, nk), 1)
    qp = past + (lax.broadcasted_iota(jnp.int32, (rows, nk), 0) & (tq - 1))
    o_s = _dot_t(_softmax_rows(s, kpos <= qp).astype(BF16), v_all)

    wb = kwb_ref.shape[2]
    kw_all = jnp.concatenate([kwb_ref[0].astype(BF16), new_cols(2)], axis=1)
    vw_all = jnp.concatenate([vwb_ref[0].astype(BF16), new_cols(3)], axis=1)
    nw = wb + NEW_PAD
    s = _dot(qr, kw_all)
    kpos = (past - wb) + lax.broadcasted_iota(jnp.int32, (rows, nw), 1)
    qp = past + (lax.broadcasted_iota(jnp.int32, (rows, nw), 0) & (tq - 1))
    mask = (kpos >= 0) & (kpos <= qp) & (kpos > qp - WINDOW)
    o_w = _dot_t(_softmax_rows(s, mask).astype(BF16), vw_all)

    outs = [_unstack_o(o, tq, gmask) for o in (o_c, o_s, o_w)]
    o_ref[0] = _gate_mix(gate_ref[0], gexp_ref, outs).astype(BF16)

    last = lax.broadcasted_iota(jnp.int32, (KV_W, 128), 1) >= 128 - n_new
    fill = jnp.zeros((128 - tq, KV_W), F32)
    for new_ref, buf_ref, out_ref in ((kwn_ref, kwb_ref, kwo_ref), (vwn_ref, vwb_ref, vwo_ref)):
        tail = jnp.concatenate([fill, pltpu.roll(new_ref[0], tq - n_new, 0)], axis=0)
        hi, lo = _split_bf16(tail)
        tail_t = _dot_t(eye, hi) + _dot_t(eye, lo)
        shifted = pltpu.roll(buf_ref[0], wb - n_new, 1)
        out_ref[0, :, 0:wb - 128] = shifted[:, 0:wb - 128]
        out_ref[0, :, wb - 128:wb] = jnp.where(last, tail_t, shifted[:, wb - 128:wb])


def attn_sample(pool_ks, pool_vs, page0, page_table, qq, gate, ck, cv, kvn, kwn, vwn, kwb, vwb, win0, n_new,
                cov_t, e_t, gexp, eye):
    nb, n_pages = page_table.shape
    page = pool_ks.shape[2]
    wb = kwb.shape[2]
    n_cmp = ck.shape[1]
    assert page == 128 and wb % 128 == 0

    def page_spec(p):
        return pl.BlockSpec((1, KV_W, page), lambda b, pt: (page0 + pt[b * n_pages + p], 0, 0))

    def bspec(shape):
        return pl.BlockSpec((1,) + shape, lambda b, pt: (b, 0, 0))

    def cspec(shape):
        return pl.BlockSpec(shape, lambda b, pt: (0,) * len(shape))

    win_spec = pl.BlockSpec((1, KV_W, wb), lambda b, pt: (win0 + b, 0, 0))
    grid_spec = pltpu.PrefetchScalarGridSpec(
        num_scalar_prefetch=1,
        grid=(nb,),
        in_specs=[page_spec(p) for p in range(n_pages)] * 2 + [
            bspec((TQ_S, 2 * Q_COLS)), bspec((TQ_S, GATE_PAD)), bspec((n_cmp, KV_W)), bspec((n_cmp, KV_W)),
            bspec((TQ_S, 4 * KV_W)), bspec((TQ_S, KV_W)), bspec((TQ_S, KV_W)),
            win_spec, win_spec,
            cspec(cov_t.shape), cspec(e_t.shape), cspec(gexp.shape), cspec(eye.shape),
        ],
        out_specs=(bspec((TQ_S, Q_COLS)), bspec((KV_W, wb)), bspec((KV_W, wb))),
        scratch_shapes=[pltpu.VMEM((MAX_SEL_BLOCKS, KV_GROUPS * TQ_S), F32)],
    )
    return pl.pallas_call(
        functools.partial(_attn_sample_body, n_pages, page, n_new),
        out_shape=(jax.ShapeDtypeStruct((nb, TQ_S, Q_COLS), BF16),
                   jax.ShapeDtypeStruct((nb, KV_W, wb), F32),
                   jax.ShapeDtypeStruct((nb, KV_W, wb), F32)),
        grid_spec=grid_spec,
        compiler_params=_cparams("parallel"),
        name="attn_sample",
    )(page_table.reshape(-1), *([pool_ks] * n_pages), *([pool_vs] * n_pages),
      qq, gate, ck, cv, kvn, kwn, vwn, kwb, vwb, cov_t, e_t, gexp, eye)


def _rope_tables(pos):
    half = HEAD_DIM // 2
    inv = ROPE_THETA ** (-jnp.arange(half, dtype=F32) / half)
    ang = pos.astype(F32)[:, None] * inv[None, :]
    c, s = jnp.cos(ang), jnp.sin(ang)
    return jnp.tile(jnp.concatenate([c, c], axis=1), (1, 2)), jnp.tile(jnp.concatenate([-s, s], axis=1), (1, 2))


def _head_perm():
    c = np.arange(Q_COLS)
    r, g, d = c // KV_W, (c % KV_W) // HEAD_DIM, c % HEAD_DIM
    return (g * GROUP_SIZE + r) * HEAD_DIM + d


def _cover_t(n_cmp, n_cmp_pad):
    cs = np.arange(n_cmp_pad) * CMP_STRIDE
    ss = np.arange(MAX_SEL_BLOCKS) * SEL_BLOCK
    m = (cs[None, :] < ss[:, None] + SEL_BLOCK) & (cs[None, :] + CMP_BLOCK > ss[:, None])
    m = m & (np.arange(n_cmp_pad)[None, :] < n_cmp)
    return jnp.asarray(m, dtype=BF16)


def _block_onehot_t(n_keys):
    k = np.arange(n_keys)
    return jnp.asarray((k[:, None] // SEL_BLOCK) == np.arange(128)[None, :], dtype=BF16)


def _gate_expand():
    ge = np.zeros((N_BRANCH, GATE_PAD, Q_COLS), np.float32)
    c = np.arange(Q_COLS)
    r, g = c // KV_W, (c % KV_W) // HEAD_DIM
    for br in range(N_BRANCH):
        ge[br, (g * GROUP_SIZE + r) * N_BRANCH + br, c] = 1.0
    return jnp.asarray(ge, dtype=BF16)


def _compress_weights(pe, w1, w2):
    half = CMP_BLOCK // 2 * HEAD_DIM
    w1cat = jnp.concatenate([w1[:half], w1[half:]], axis=1).astype(BF16)
    pe2 = jnp.pad(pe.reshape(2, half), ((0, 6), (0, 0)))
    w2bd = jnp.kron(jnp.eye(KV_GROUPS, dtype=F32), w2).astype(BF16)
    return w1cat, pe2, w2bd


def _row(v):
    return v.reshape(1, -1)


TM_PROMPT = 512
TM_SAMPLE = 512
TM_FFN_PROMPT = 1024


def kernel(x_prompt, x_sample, state_conv, cache_k_cmp, cache_v_cmp, cache_k_sel, cache_v_sel, state_k_win, state_v_win, page_table, p_prompt, p_sample, norm_mix_pre, norm_mix_post, norm_ffn_pre, norm_ffn_post, conv_w_in, conv_b_in, conv_w_dw, conv_b_dw, conv_ln_g, conv_ln_b, conv_w_out, conv_b_out, nsa_w_in, nsa_b_gate, nsa_pe_k, nsa_pe_v, nsa_wk_c1, nsa_wk_c2, nsa_wv_c1, nsa_wv_c2, nsa_w_o, ffn_w_in, ffn_w_out, ple_w_gate, ple_w_proj):
    n_p, seq, _ = x_prompt.shape
    n_s, n_new, _ = x_sample.shape
    n_pages = page_table.shape[1]
    page = cache_k_cmp.shape[2]
    past = n_pages * page
    tp, ts = n_p * seq, n_s * n_new
    assert seq % KEY_CHUNK == 0 and seq // SEL_BLOCK <= MAX_SEL_BLOCKS and seq >= WINDOW + TQ_P
    assert (past + n_new + SEL_BLOCK - 1) // SEL_BLOCK <= MAX_SEL_BLOCKS and n_new <= TQ_S
    assert past % SEL_BLOCK == 0 and page % CMP_STRIDE == 0 and state_k_win.shape[2] == WINDOW

    hp = x_prompt.reshape(tp, D_MODEL)
    hs = x_sample.reshape(ts, D_MODEL)
    pp = p_prompt.reshape(DEPTH, tp, -1)
    ps = p_sample.reshape(DEPTH, ts, -1)

    perm = _head_perm()
    gexp = _gate_expand()
    cos_p, sin_p = _rope_tables(jnp.arange(seq))
    cos_s, sin_s = _rope_tables(past + jnp.arange(ts) % n_new)
    n_cmp_p = seq // CMP_STRIDE - 1
    cov_p = _cover_t(n_cmp_p, seq // CMP_STRIDE)
    n_cmp_s = past // CMP_STRIDE
    cov_s = _cover_t(n_cmp_s, n_cmp_s)
    et_p = _block_onehot_t(seq)
    et_s = _block_onehot_t(past + NEW_PAD)
    eye = jnp.eye(KV_W, dtype=BF16)
    tiles_per_seq = seq // TM_PROMPT

    def pad_rows(x, n):
        return jnp.pad(x, ((0, 0), (0, n - x.shape[1]), (0, 0)))

    def token_minor(c):
        return jnp.transpose(c, (0, 1, 3, 4, 2)).reshape(c.shape[0] * c.shape[1], KV_W, c.shape[2])

    def token_major(t, lead):
        t = t.reshape(lead + (KV_GROUPS, HEAD_DIM, t.shape[-1]))
        return jnp.moveaxis(t, -1, len(lead))

    pools = [token_minor(c) for c in (cache_k_cmp, cache_v_cmp, cache_k_sel, cache_v_sel)]
    n_pool = cache_k_cmp.shape[1]
    kwin_t, vwin_t = token_minor(state_k_win), token_minor(state_v_win)
    conv_state_t = jnp.transpose(state_conv, (0, 2, 1, 3)).reshape(-1, n_s, D_MODEL)

    conv_p, conv_s, nsa_p, nsa_s = [], [], [], []
    for i in range(DEPTH):
        j = i // N_MIXERS
        g_pre, g_post = _row(norm_mix_pre[i]), _row(norm_mix_post[i])
        if i % N_MIXERS == 0:
            w_in = conv_w_in[j].astype(BF16)
            b_in = _row(conv_b_in[j])
            w_out = conv_w_out[j].astype(BF16)
            dw = (conv_w_dw[j], _row(conv_b_dw[j]), _row(conv_ln_g[j]), _row(conv_ln_b[j]))
            glu_p = conv_in(hp, g_pre, w_in, b_in, TM_PROMPT)
            glu_s = conv_in(hs, g_pre, w_in, b_in, TM_SAMPLE)
            z_p = dwconv_prompt(glu_p, n_p, seq, *dw)
            z_s, st_s = dwconv_sample(conv_state_t, j, glu_s.reshape(n_s, n_new, D_MODEL), *dw)
            conv_p.append(glu_p.reshape(n_p, seq, D_MODEL)[:, seq - (CONV_WIDTH - 1):])
            conv_s.append(st_s)
            b_out = _row(conv_b_out[j])
            hp = out_proj_res(z_p, w_out, b_out, g_post, hp, TM_PROMPT)
            hs = out_proj_res(z_s.reshape(ts, D_MODEL), w_out, b_out, g_post, hs, TM_SAMPLE)
        else:
            w = nsa_w_in[j]
            w_in = jnp.concatenate([w[:, :Q_COLS][:, perm], w[:, Q_COLS:],
                                    jnp.zeros((D_MODEL, GATE_PAD - GATE_COLS), F32)], axis=1).astype(BF16)
            bg = jnp.pad(nsa_b_gate[j], (0, GATE_PAD - GATE_COLS)).reshape(1, GATE_PAD)
            w_o = nsa_w_o[j][perm].astype(BF16)
            wk = _compress_weights(nsa_pe_k[j], nsa_wk_c1[j], nsa_wk_c2[j])
            wv = _compress_weights(nsa_pe_v[j], nsa_wv_c1[j], nsa_wv_c2[j])
            zero_b = jnp.zeros((1, D_MODEL), F32)

            qq, kc, vc, kc_t, vc_t, ks_t, vs_t, kw_t, vw_t, kvs, kvw, gate = nsa_in(
                hp, g_pre, w_in, bg, cos_p, sin_p, lambda t: t % tiles_per_seq, TM_PROMPT, seq=seq)
            zero_w2 = jnp.zeros((CMP_HIDDEN, HEAD_DIM), F32)
            w2_packed = jnp.block([[nsa_wk_c2[j], zero_w2], [zero_w2, nsa_wv_c2[j]]]).astype(BF16)
            ckv = compress_prompt(kc, vc, n_p, seq, wk, wv, w2_packed)
            o_p = attn_prompt(qq, gate, ckv, kvs, kvw, n_p, seq, cov_p, et_p, gexp)
            hp = out_proj_res(o_p, w_o, zero_b, g_post, hp, TM_PROMPT)
            keep = min(WINDOW, seq)
            nsa_p.append((kc_t, vc_t, ks_t, vs_t, kw_t[:, :, seq - keep:], vw_t[:, :, seq - keep:]))

            qq, kc, vc, ks, vs, kw, vw, kvb, gate = nsa_in(
                hs, g_pre, w_in, bg, cos_s, sin_s, lambda t: t, TM_SAMPLE)
            r3 = lambda x: x.reshape(n_s, n_new, x.shape[-1])
            ck, cv = compress_sample(pools[0], pools[1], j * n_pool, page_table,
                                     pad_rows(r3(kc), CMP_STRIDE), pad_rows(r3(vc), CMP_STRIDE), wk, wv)
            o_s, kwo, vwo = attn_sample(
                pools[2], pools[3], j * n_pool, page_table,
                pad_rows(r3(qq), TQ_S), pad_rows(r3(gate), TQ_S), ck, cv,
                pad_rows(r3(kvb), TQ_S), pad_rows(r3(kw), TQ_S), pad_rows(r3(vw), TQ_S),
                kwin_t, vwin_t, j * n_s, n_new, cov_s, et_s, gexp, eye)
            hs = out_proj_res(o_s[:, :n_new].reshape(ts, Q_COLS), w_o, zero_b, g_post, hs, TM_SAMPLE)
            kvshape = (n_s, n_new, KV_GROUPS, HEAD_DIM)
            nsa_s.append((kc.reshape(kvshape), vc.reshape(kvshape), ks.reshape(kvshape), vs.reshape(kvshape),
                          kwo, vwo))

        fw = (_row(norm_ffn_pre[i]), _row(norm_ffn_post[i]), ffn_w_in[i].astype(BF16), ffn_w_out[i].astype(BF16),
              ple_w_gate[i].astype(BF16), ple_w_proj[i].astype(BF16))
        hp = ffn_ple(hp, pp[i], *fw, TM_FFN_PROMPT)
        hs = ffn_ple(hs, ps[i], *fw, TM_SAMPLE)

    stack = lambda lst, k: jnp.stack([st[k] for st in lst])
    stack_t = lambda lst, k, nb: token_major(stack(lst, k), (len(lst), nb))
    conv_state_s = jnp.transpose(jnp.stack(conv_s), (0, 2, 1, 3))
    return (hp.reshape(x_prompt.shape), hs.reshape(x_sample.shape), jnp.stack(conv_p),
            stack_t(nsa_p, 0, n_p), stack_t(nsa_p, 1, n_p), stack_t(nsa_p, 2, n_p), stack_t(nsa_p, 3, n_p),
            stack_t(nsa_p, 4, n_p), stack_t(nsa_p, 5, n_p),
            conv_state_s,
            stack(nsa_s, 0), stack(nsa_s, 1), stack(nsa_s, 2), stack(nsa_s, 3),
            stack_t(nsa_s, 4, n_s), stack_t(nsa_s, 5, n_s))
```

```python
import functools

import numpy as np
import jax
import jax.numpy as jnp
from jax import lax
from jax.experimental import pallas as pl
from jax.experimental.pallas import tpu as pltpu

F32 = jnp.float32
BF16 = jnp.bfloat16

D_MODEL = 1024
DEPTH = 4
N_MIXERS = 2
CONV_WIDTH = 31
N_HEADS = 16
HEAD_DIM = 64
KV_GROUPS = 4
GROUP_SIZE = 4
N_BRANCH = 3
CMP_BLOCK = 32
CMP_STRIDE = 16
CMP_HIDDEN = 2 * HEAD_DIM
SEL_BLOCK = 64
N_SELECT = 16
WINDOW = 512
D_FF = 4 * D_MODEL
ROPE_THETA = 10000.0
EPS = 1e-6
KV_W = KV_GROUPS * HEAD_DIM
Q_COLS = N_HEADS * HEAD_DIM
KV_COLS = 2 * N_BRANCH * KV_W
GATE_COLS = N_BRANCH * N_HEADS
GATE_PAD = 128
NEG = -1e30
LOG2E = 1.4426950408889634
MAX_SEL_BLOCKS = 64
VMEM_LIMIT = 56 * 1024 * 1024


def _cparams(*sem):
    return pltpu.CompilerParams(dimension_semantics=sem, vmem_limit_bytes=VMEM_LIMIT)


def _rms(x, g):
    return x * lax.rsqrt(jnp.mean(x * x, axis=-1, keepdims=True) + EPS) * g


def _dot(a, b):
    return jnp.dot(a, b, preferred_element_type=F32)


def _dot_t(a, b):
    return lax.dot_general(a, b, (((1,), (1,)), ((), ())), preferred_element_type=F32)


def _split_bf16(x):
    hi = x.astype(BF16)
    lo = (x - hi.astype(F32)).astype(BF16)
    return hi, lo


COL_TILE = 256


def _conv_in_body(x_ref, g_ref, w_ref, b_ref, o_ref):
    u = _rms(x_ref[...], g_ref[...]).astype(BF16)
    for j in range(D_MODEL // COL_TILE):
        lo, hi = j * COL_TILE, (j + 1) * COL_TILE
        a1 = _dot(u, w_ref[:, lo:hi]) + b_ref[:, lo:hi]
        a2 = _dot(u, w_ref[:, D_MODEL + lo:D_MODEL + hi]) + b_ref[:, D_MODEL + lo:D_MODEL + hi]
        o_ref[:, lo:hi] = a1 * jax.nn.sigmoid(a2)


def conv_in(h, g, w, b, tm):
    m = h.shape[0]
    return pl.pallas_call(
        _conv_in_body,
        out_shape=jax.ShapeDtypeStruct((m, D_MODEL), F32),
        grid=(m // tm,),
        in_specs=[
            pl.BlockSpec((tm, D_MODEL), lambda i: (i, 0)),
            pl.BlockSpec((1, D_MODEL), lambda i: (0, 0)),
            pl.BlockSpec((D_MODEL, 2 * D_MODEL), lambda i: (0, 0)),
            pl.BlockSpec((1, 2 * D_MODEL), lambda i: (0, 0)),
        ],
        out_specs=pl.BlockSpec((tm, D_MODEL), lambda i: (i, 0)),
        compiler_params=_cparams("parallel"),
        name="conv_in",
    )(h, g, w, b)


HALO = 32
DW_ROWS = 64
DW_LANES = 256


def _ln_silu(y, ln_g, ln_b):
    mu = jnp.mean(y, axis=-1, keepdims=True)
    yc = y - mu
    var = jnp.mean(yc * yc, axis=-1, keepdims=True)
    yn = yc * lax.rsqrt(var + EPS) * ln_g + ln_b
    return yn * jax.nn.sigmoid(yn)


def _dwconv_prompt_body(ts, cur_ref, halo_ref, w_ref, bdw_ref, lng_ref, lnb_ref, z_ref, xin_ref, y_ref):
    first = pl.program_id(1) == 0

    @pl.when(first)
    def _():
        xin_ref[0:HALO, :] = jnp.zeros((HALO, D_MODEL), F32)

    @pl.when(jnp.logical_not(first))
    def _():
        xin_ref[0:HALO, :] = halo_ref[...]

    xin_ref[HALO:HALO + ts, :] = cur_ref[...]
    xin_ref[HALO + ts:HALO + ts + 8, :] = jnp.zeros((8, D_MODEL), F32)
    off = HALO - (CONV_WIDTH - 1)
    for rc in range(ts // DW_ROWS):
        for lc in range(D_MODEL // DW_LANES):
            l0, l1 = lc * DW_LANES, (lc + 1) * DW_LANES
            acc = jnp.zeros((DW_ROWS, DW_LANES), F32)
            for b in range(8):
                z = None
                for a in range((off + CONV_WIDTH + 7) // 8):
                    k = 8 * a + b - off
                    if 0 <= k < CONV_WIDTH:
                        r0 = rc * DW_ROWS + 8 * a
                        term = xin_ref[r0:r0 + DW_ROWS + 8, l0:l1] * w_ref[k:k + 1, l0:l1]
                        z = term if z is None else z + term
                acc = acc + z[b:b + DW_ROWS]
            y_ref[rc * DW_ROWS:(rc + 1) * DW_ROWS, l0:l1] = acc
    y = y_ref[...] + bdw_ref[...]
    z_ref[...] = _ln_silu(y, lng_ref[...], lnb_ref[...]).astype(BF16)


def dwconv_prompt(glu, n_seq, seq, w_dw, b_dw, ln_g, ln_b, ts=256):
    nt = seq // ts
    hb = ts // HALO
    return pl.pallas_call(
        functools.partial(_dwconv_prompt_body, ts),
        out_shape=jax.ShapeDtypeStruct((n_seq * seq, D_MODEL), BF16),
        grid=(n_seq, nt),
        in_specs=[
            pl.BlockSpec((ts, D_MODEL), lambda b, i: (b * nt + i, 0)),
            pl.BlockSpec((HALO, D_MODEL), lambda b, i: (jnp.maximum((b * nt + i) * hb - 1, 0), 0)),
            pl.BlockSpec((CONV_WIDTH, D_MODEL), lambda b, i: (0, 0)),
            pl.BlockSpec((1, D_MODEL), lambda b, i: (0, 0)),
            pl.BlockSpec((1, D_MODEL), lambda b, i: (0, 0)),
            pl.BlockSpec((1, D_MODEL), lambda b, i: (0, 0)),
        ],
        out_specs=pl.BlockSpec((ts, D_MODEL), lambda b, i: (b * nt + i, 0)),
        scratch_shapes=[pltpu.VMEM((HALO + ts + 8, D_MODEL), F32), pltpu.VMEM((ts, D_MODEL), F32)],
        compiler_params=_cparams("parallel", "parallel"),
        name="dwconv_prompt",
    )(glu, glu, w_dw, b_dw, ln_g, ln_b)


def _dwconv_sample_body(n_new, st_ref, glu_ref, w_ref, bdw_ref, lng_ref, lnb_ref, z_ref, ns_ref):
    n_st = CONV_WIDTH - 1
    rows = [st_ref[j] for j in range(n_st)] + [glu_ref[:, j, :] for j in range(n_new)]
    for t in range(n_new):
        acc = rows[t] * w_ref[0:1, :]
        for k in range(1, CONV_WIDTH):
            acc = acc + rows[t + k] * w_ref[k:k + 1, :]
        y = acc + bdw_ref[...]
        z_ref[:, t, :] = _ln_silu(y, lng_ref[...], lnb_ref[...])
    for j in range(n_st):
        ns_ref[j] = rows[j + n_new]


def dwconv_sample(state, layer, glu, w_dw, b_dw, ln_g, ln_b, bb=8):
    nb, n_new, _ = glu.shape
    n_st = CONV_WIDTH - 1
    vec = pl.BlockSpec((1, D_MODEL), lambda i: (0, 0))
    return pl.pallas_call(
        functools.partial(_dwconv_sample_body, n_new),
        out_shape=(jax.ShapeDtypeStruct((nb, n_new, D_MODEL), F32),
                   jax.ShapeDtypeStruct((n_st, nb, D_MODEL), F32)),
        grid=(nb // bb,),
        in_specs=[
            pl.BlockSpec((n_st, bb, D_MODEL), lambda i: (layer, i, 0)),
            pl.BlockSpec((bb, n_new, D_MODEL), lambda i: (i, 0, 0)),
            pl.BlockSpec((CONV_WIDTH, D_MODEL), lambda i: (0, 0)),
            vec, vec, vec,
        ],
        out_specs=(pl.BlockSpec((bb, n_new, D_MODEL), lambda i: (i, 0, 0)),
                   pl.BlockSpec((n_st, bb, D_MODEL), lambda i: (0, i, 0))),
        compiler_params=_cparams("parallel"),
        name="dwconv_sample",
    )(state, glu, w_dw, b_dw, ln_g, ln_b)


def _out_proj_body(z_ref, w_ref, b_ref, g_ref, h_ref, o_ref):
    m = _dot(z_ref[...].astype(BF16), w_ref[...]) + b_ref[...]
    o_ref[...] = h_ref[...] + _rms(m, g_ref[...])


def out_proj_res(z, w, b, g, h, tm):
    m = h.shape[0]
    vec = pl.BlockSpec((1, D_MODEL), lambda i: (0, 0))
    row = pl.BlockSpec((tm, D_MODEL), lambda i: (i, 0))
    return pl.pallas_call(
        _out_proj_body,
        out_shape=jax.ShapeDtypeStruct((m, D_MODEL), F32),
        grid=(m // tm,),
        in_specs=[row, pl.BlockSpec((D_MODEL, D_MODEL), lambda i: (0, 0)), vec, vec, row],
        out_specs=row,
        compiler_params=_cparams("parallel"),
        name="out_proj_res",
    )(z, w, b, g, h)


FF_CHUNK = 1024


def _ffn_body(h_ref, p_ref, gpre_ref, gpost_ref, win_ref, wout_ref, wpg_ref, wpp_ref, o_ref, u_ref, acc_ref):
    j = pl.program_id(1)

    @pl.when(j == 0)
    def _():
        u_ref[...] = _rms(h_ref[...], gpre_ref[...]).astype(BF16)
        acc_ref[...] = jnp.zeros_like(acc_ref)

    a = _dot(u_ref[...], win_ref[...])
    a = jnp.square(jnp.maximum(a, 0.0)).astype(BF16)
    acc_ref[...] += _dot(a, wout_ref[...])

    @pl.when(j == pl.num_programs(1) - 1)
    def _():
        h2 = h_ref[...] + _rms(acc_ref[...], gpost_ref[...])
        gate = jax.nn.sigmoid(_dot(h2.astype(BF16), wpg_ref[...]))
        proj = _dot(p_ref[...].astype(BF16), wpp_ref[...])
        o_ref[...] = h2 + gate * proj


def ffn_ple(h, p, g_pre, g_post, w_in, w_out, w_pg, w_pp, tm):
    m = h.shape[0]
    ple = p.shape[1]
    vec = pl.BlockSpec((1, D_MODEL), lambda i, j: (0, 0))
    row = pl.BlockSpec((tm, D_MODEL), lambda i, j: (i, 0))
    return pl.pallas_call(
        _ffn_body,
        out_shape=jax.ShapeDtypeStruct((m, D_MODEL), F32),
        grid=(m // tm, D_FF // FF_CHUNK),
        in_specs=[
            row,
            pl.BlockSpec((tm, ple), lambda i, j: (i, 0)),
            vec, vec,
            pl.BlockSpec((D_MODEL, FF_CHUNK), lambda i, j: (0, j)),
            pl.BlockSpec((FF_CHUNK, D_MODEL), lambda i, j: (j, 0)),
            pl.BlockSpec((D_MODEL, D_MODEL), lambda i, j: (0, 0)),
            pl.BlockSpec((ple, D_MODEL), lambda i, j: (0, 0)),
        ],
        out_specs=row,
        scratch_shapes=[pltpu.VMEM((tm, D_MODEL), BF16), pltpu.VMEM((tm, D_MODEL), F32)],
        compiler_params=_cparams("parallel", "arbitrary"),
        name="ffn_ple",
    )(h, p, g_pre, g_post, w_in, w_out, w_pg, w_pp)


NSA_W_COLS = Q_COLS + KV_COLS + GATE_PAD


def _pack_groups(k, v, out_ref):
    low = lax.broadcasted_iota(jnp.int32, (k.shape[0], 128), 1) < HEAD_DIM
    for pair in range(KV_GROUPS // 2):
        kp = k[:, pair * 128:(pair + 1) * 128]
        vp = v[:, pair * 128:(pair + 1) * 128]
        out_ref[0, 2 * pair] = jnp.where(low, kp, pltpu.roll(vp, HEAD_DIM, 1)).astype(BF16)
        out_ref[0, 2 * pair + 1] = jnp.where(low, pltpu.roll(kp, HEAD_DIM, 1), vp).astype(BF16)


def _nsa_in_body(transposed, n_prev, x_ref, g_ref, w_ref, bg_ref, cos_ref, sin_ref, *rest):
    qq_ref, out_refs = rest[n_prev], rest[n_prev + 1:]
    if transposed:
        kc_ref, vc_ref = out_refs[0:2]
        t_refs = out_refs[2:8]
        outs = (kc_ref, vc_ref, None, None, None, None)
        packed_refs = out_refs[8:10]
        kvb_ref = None
    else:
        outs = out_refs[0:6]
        t_refs = None
        kvb_ref = out_refs[6]
    gate_ref = out_refs[-1]
    u = _rms(x_ref[...], g_ref[...]).astype(BF16)
    tm = u.shape[0]
    cos = jnp.concatenate([cos_ref[...], cos_ref[...]], axis=1)
    sin = jnp.concatenate([sin_ref[...], sin_ref[...]], axis=1)
    lane = lax.broadcasted_iota(jnp.int32, (tm, KV_W), 1)
    first_half = (lane & (HEAD_DIM - 1)) < (HEAD_DIM // 2)

    def rope(x):
        rot = jnp.where(first_half, pltpu.roll(x, KV_W - HEAD_DIM // 2, 1), pltpu.roll(x, HEAD_DIM // 2, 1))
        return x * cos + rot * sin

    scale = HEAD_DIM ** -0.5 * LOG2E
    for j in range(Q_COLS // KV_W):
        q = _dot(u, w_ref[:, j * KV_W:(j + 1) * KV_W]) * scale
        qq_ref[:, j * KV_W:(j + 1) * KV_W] = q.astype(BF16)
        qq_ref[:, Q_COLS + j * KV_W:Q_COLS + (j + 1) * KV_W] = rope(q).astype(BF16)
    for j in range(2 * N_BRANCH):
        x = _dot(u, w_ref[:, Q_COLS + j * KV_W:Q_COLS + (j + 1) * KV_W])
        if j in (2, 4):
            x = rope(x)
        if outs[j] is not None:
            outs[j][...] = x
        if transposed:
            t_refs[j][0, 0] = jnp.transpose(x)
            if j in (2, 4):
                k_rows = x
            if j in (3, 5):
                _pack_groups(k_rows, x, packed_refs[(j - 3) // 2])
        elif j >= 2:
            kvb_ref[:, (j - 2) * KV_W:(j - 1) * KV_W] = x.astype(BF16)
    gl = _dot(u, w_ref[:, Q_COLS + KV_COLS:]) + bg_ref[...]
    gate_ref[...] = jax.nn.sigmoid(gl)


def nsa_in(h, g, w, bg, cos_t, sin_t, pos_map, tm, seq=None, layer=0, n_layers=1, prev=()):
    m = h.shape[0]
    kv_spec = pl.BlockSpec((tm, KV_W), lambda i: (i, 0))
    kv_shape = jax.ShapeDtypeStruct((m, KV_W), F32)
    if seq is None:
        kv_shapes = (kv_shape,) * 6 + (jax.ShapeDtypeStruct((m, 4 * KV_W), BF16),)
        kv_specs = (kv_spec,) * 6 + (pl.BlockSpec((tm, 4 * KV_W), lambda i: (i, 0)),)
    else:
        nt = seq // tm
        t_shape = jax.ShapeDtypeStruct((n_layers, m // seq, KV_W, seq), F32)
        t_spec = pl.BlockSpec((1, 1, KV_W, tm), lambda i: (layer, i // nt, 0, i % nt))
        p_shape = jax.ShapeDtypeStruct((m // seq, KV_GROUPS, seq, 128), BF16)
        p_spec = pl.BlockSpec((1, KV_GROUPS, tm, 128), lambda i: (i // nt, 0, i % nt, 0))
        kv_shapes = (kv_shape,) * 2 + (t_shape,) * 6 + (p_shape,) * 2
        kv_specs = (kv_spec,) * 2 + (t_spec,) * 6 + (p_spec,) * 2
    n_in = 6
    return pl.pallas_call(
        functools.partial(_nsa_in_body, seq is not None, len(prev)),
        out_shape=(jax.ShapeDtypeStruct((m, 2 * Q_COLS), BF16),) + kv_shapes + (
                   jax.ShapeDtypeStruct((m, GATE_PAD), F32),),
        grid=(m // tm,),
        in_specs=[
            pl.BlockSpec((tm, D_MODEL), lambda i: (i, 0)),
            pl.BlockSpec((1, D_MODEL), lambda i: (0, 0)),
            pl.BlockSpec((D_MODEL, NSA_W_COLS), lambda i: (0, 0)),
            pl.BlockSpec((1, GATE_PAD), lambda i: (0, 0)),
            pl.BlockSpec((tm, 2 * HEAD_DIM), lambda i: (pos_map(i), 0)),
            pl.BlockSpec((tm, 2 * HEAD_DIM), lambda i: (pos_map(i), 0)),
        ] + [pl.BlockSpec(memory_space=pl.ANY)] * len(prev),
        out_specs=(pl.BlockSpec((tm, 2 * Q_COLS), lambda i: (i, 0)),) + kv_specs + (
                   pl.BlockSpec((tm, GATE_PAD), lambda i: (i, 0)),),
        input_output_aliases={n_in + k: 3 + k for k in range(len(prev))},
        compiler_params=_cparams("parallel"),
        name="nsa_in",
    )(h, g, w, bg, cos_t, sin_t, *prev)


CHUNK_W = CMP_STRIDE * HEAD_DIM
TOK_PAIRS = CMP_STRIDE // 2


def _compress_one(load_t, nc, n_out, w1_ref, pe_ref, w2_ref, out_ref, c_ref, ab_ref):
    lane = lax.broadcasted_iota(jnp.int32, (nc, 2 * HEAD_DIM), 1)
    low = lane < HEAD_DIM
    for tp in range(TOK_PAIRS):
        for half in range(KV_GROUPS // 2):
            a = load_t(2 * tp, half)
            b = load_t(2 * tp + 1, half)
            even = jnp.where(low, a, pltpu.roll(b, HEAD_DIM, 1))
            odd = jnp.where(low, pltpu.roll(a, HEAD_DIM, 1), b)
            g0 = 2 * half
            c_ref[g0 * nc:(g0 + 1) * nc, tp * 128:(tp + 1) * 128] = even.astype(BF16)
            c_ref[(g0 + 1) * nc:(g0 + 2) * nc, tp * 128:(tp + 1) * 128] = odd.astype(BF16)
    ab_ref[0:KV_GROUPS * nc, :] = _dot(c_ref[...], w1_ref[...])
    ab_ref[KV_GROUPS * nc:KV_GROUPS * nc + 8, :] = jnp.zeros((8, 2 * CMP_HIDDEN), F32)
    pe2 = _dot(pe_ref[...].astype(BF16), w1_ref[...])
    pe_c = pe2[0:1, 0:CMP_HIDDEN] + pe2[1:2, CMP_HIDDEN:2 * CMP_HIDDEN]
    hs = []
    for g in range(KV_GROUPS):
        top = ab_ref[g * nc:g * nc + n_out, 0:CMP_HIDDEN]
        bot = ab_ref[g * nc + 1:g * nc + 1 + n_out, CMP_HIDDEN:2 * CMP_HIDDEN]
        hs.append(jax.nn.gelu(top + bot + pe_c).astype(BF16))
    if out_ref is None:
        return hs
    hcat = jnp.concatenate(hs, axis=1)
    out_ref[0] = _dot(hcat, w2_ref[...]).astype(BF16)


def _compress_prompt_body(nc, k0_ref, k1_ref, v0_ref, v1_ref, wk1_ref, pek_ref, wv1_ref, pev_ref, w2_ref,
                          ckv_ref, c_ref, ab_ref):
    def loader(halves):
        return lambda t, half: halves[half][pl.ds(t, nc, stride=CMP_STRIDE), :]

    hk = _compress_one(loader((k0_ref, k1_ref)), nc, nc, wk1_ref, pek_ref, None, None, c_ref, ab_ref)
    hv = _compress_one(loader((v0_ref, v1_ref)), nc, nc, wv1_ref, pev_ref, None, None, c_ref, ab_ref)
    for g in range(KV_GROUPS):
        ckv_ref[0, g] = _dot(jnp.concatenate([hk[g], hv[g]], axis=1), w2_ref[...]).astype(BF16)


def _compress_weight_specs(idx):
    return [pl.BlockSpec((CHUNK_W, 2 * CMP_HIDDEN), idx),
            pl.BlockSpec((8, CHUNK_W), idx),
            pl.BlockSpec((KV_GROUPS * CMP_HIDDEN, KV_W), idx)]


def compress_prompt(kc, vc, n_seq, seq, wk, wv, w2_packed):
    nc = seq // CMP_STRIDE
    idx = lambda b: (0, 0)
    halves = [pl.BlockSpec((seq, 128), lambda b: (b, 0)), pl.BlockSpec((seq, 128), lambda b: (b, 1))]
    w1_specs = _compress_weight_specs(idx)[0:2]
    return pl.pallas_call(
        functools.partial(_compress_prompt_body, nc),
        out_shape=jax.ShapeDtypeStruct((n_seq, KV_GROUPS, nc, 128), BF16),
        grid=(n_seq,),
        in_specs=halves + halves + w1_specs + w1_specs + [pl.BlockSpec(w2_packed.shape, idx)],
        out_specs=pl.BlockSpec((1, KV_GROUPS, nc, 128), lambda b: (b, 0, 0, 0)),
        scratch_shapes=[pltpu.VMEM((KV_GROUPS * nc, CHUNK_W), BF16),
                        pltpu.VMEM((KV_GROUPS * nc + 8, 2 * CMP_HIDDEN), F32)],
        compiler_params=_cparams("parallel"),
        name="compress_prompt",
    )(kc, kc, vc, vc, wk[0], wk[1], wv[0], wv[1], w2_packed)


def _compress_sample_body(n_pages, page, nc, n_out, pt_ref, *refs):
    k_pages = refs[0:n_pages]
    v_pages = refs[n_pages:2 * n_pages]
    k_new, v_new = refs[2 * n_pages:2 * n_pages + 2]
    wk1_ref, pek_ref, wk2_ref, wv1_ref, pev_ref, wv2_ref = refs[2 * n_pages + 2:2 * n_pages + 8]
    ck_ref, cv_ref, c_ref, ab_ref, x0_ref, x1_ref = refs[2 * n_pages + 8:]
    past = n_pages * page
    n_rows = nc * CMP_STRIDE
    halves = (x0_ref, x1_ref)

    def run(pages, new_ref, w1_ref, pe_ref, w2_ref, out_ref):
        for p, page_ref in enumerate(pages):
            for half in range(2):
                halves[half][p * page:(p + 1) * page, :] = jnp.transpose(page_ref[0, half * 128:(half + 1) * 128, :])
        for half in range(2):
            halves[half][past:past + CMP_STRIDE, :] = new_ref[0, :, half * 128:(half + 1) * 128]
            halves[half][past + CMP_STRIDE:n_rows, :] = jnp.zeros((n_rows - past - CMP_STRIDE, 128), F32)
        load_t = lambda t, half: halves[half][pl.ds(t, nc, stride=CMP_STRIDE), :]
        _compress_one(load_t, nc, n_out, w1_ref, pe_ref, w2_ref, out_ref, c_ref, ab_ref)

    run(k_pages, k_new, wk1_ref, pek_ref, wk2_ref, ck_ref)
    run(v_pages, v_new, wv1_ref, pev_ref, wv2_ref, cv_ref)


def compress_sample(pool_k, pool_v, page0, page_table, k_new, v_new, wk, wv):
    nb, n_pages = page_table.shape
    page = pool_k.shape[2]
    assert page == 128
    n_chunk = n_pages * (page // CMP_STRIDE) + 1
    n_out = n_chunk - 1
    nc = -(-(n_chunk + 1) // 16) * 16
    idx = lambda b, pt: (0, 0)
    page_specs = [pl.BlockSpec((1, KV_W, page), functools.partial(lambda p, b, pt: (page0 + pt[b * n_pages + p], 0, 0), p))
                  for p in range(n_pages)]
    new_spec = pl.BlockSpec((1, CMP_STRIDE, KV_W), lambda b, pt: (b, 0, 0))
    out = jax.ShapeDtypeStruct((nb, n_out, KV_W), BF16)
    grid_spec = pltpu.PrefetchScalarGridSpec(
        num_scalar_prefetch=1,
        grid=(nb,),
        in_specs=page_specs * 2 + [new_spec, new_spec]
        + _compress_weight_specs(idx) + _compress_weight_specs(idx),
        out_specs=(pl.BlockSpec((1, n_out, KV_W), lambda b, pt: (b, 0, 0)),
                   pl.BlockSpec((1, n_out, KV_W), lambda b, pt: (b, 0, 0))),
        scratch_shapes=[pltpu.VMEM((KV_GROUPS * nc, CHUNK_W), BF16),
                        pltpu.VMEM((KV_GROUPS * nc + 8, 2 * CMP_HIDDEN), F32),
                        pltpu.VMEM((nc * CMP_STRIDE, 128), F32),
                        pltpu.VMEM((nc * CMP_STRIDE, 128), F32)],
    )
    return pl.pallas_call(
        functools.partial(_compress_sample_body, n_pages, page, nc, n_out),
        out_shape=(out, out),
        grid_spec=grid_spec,
        compiler_params=_cparams("parallel"),
        name="compress_sample",
    )(page_table.reshape(-1), *([pool_k] * n_pages), *([pool_v] * n_pages), k_new, v_new, *wk, *wv)


def _group_masks():
    lane = lax.broadcasted_iota(jnp.int32, (1, KV_W), 1)
    return [(lane // HEAD_DIM) == g for g in range(KV_GROUPS)]


def _stack_q(q, gmask):
    if q.shape[0] % 16:
        q = q.astype(F32)
    zero = jnp.zeros((q.shape[0], KV_W), q.dtype)
    return jnp.concatenate([jnp.where(gmask[g], q[:, r * KV_W:(r + 1) * KV_W], zero)
                            for g in range(KV_GROUPS) for r in range(GROUP_SIZE)], axis=0).astype(BF16)


def _unstack_o(o, tq, gmask):
    cols = []
    for r in range(GROUP_SIZE):
        acc = jnp.zeros((tq, KV_W), F32)
        for g in range(KV_GROUPS):
            row = (g * GROUP_SIZE + r) * tq
            acc = acc + jnp.where(gmask[g], o[row:row + tq, :], 0.0)
        cols.append(acc)
    return jnp.concatenate(cols, axis=1)


def _rep_rows(x, tq):
    return jnp.concatenate([x[g * tq:(g + 1) * tq] for g in range(KV_GROUPS) for _ in range(GROUP_SIZE)], axis=0)


def _softmax_rows(s, mask):
    s = jnp.where(mask, s, NEG)
    m = jnp.max(s, axis=1, keepdims=True)
    e = jnp.where(mask, jnp.exp2(s - m), 0.0)
    d = jnp.sum(e, axis=1, keepdims=True)
    return e * (1.0 / jnp.where(d > 0, d, 1.0))


def _cmp_branch(qc, ck, cv, qpos0, tq):
    rows, n = qc.shape[0], ck.shape[0]
    s = _dot_t(qc, ck)
    col = lax.broadcasted_iota(jnp.int32, (rows, n), 1)
    qp = qpos0 + (lax.broadcasted_iota(jnp.int32, (rows, n), 0) & (tq - 1))
    p = _softmax_rows(s, col * CMP_STRIDE + (CMP_BLOCK - 1) <= qp)
    o = _dot(p.astype(BF16), cv)
    psum = []
    for g in range(KV_GROUPS):
        acc = p[g * GROUP_SIZE * tq:(g * GROUP_SIZE + 1) * tq]
        for r in range(1, GROUP_SIZE):
            acc = acc + p[(g * GROUP_SIZE + r) * tq:(g * GROUP_SIZE + r + 1) * tq]
        psum.append(acc)
    return o, jnp.concatenate(psum, axis=0)


def _select_bias(psum, cov_t, qpos0, tq, n_live, score_ref):
    hi, lo = _split_bf16(psum)
    imp = _dot_t(cov_t, hi) + _dot_t(cov_t, lo)
    shape = imp.shape
    blk = lax.broadcasted_iota(jnp.int32, shape, 0)
    cur = (qpos0 + (lax.broadcasted_iota(jnp.int32, shape, 1) & (tq - 1))) // SEL_BLOCK
    forced = (blk == 0) | (blk == cur) | (blk == cur - 1)
    score = jnp.where(blk > cur, -jnp.inf, jnp.where(forced, jnp.inf, imp))
    score_ref[...] = score

    def count(k, rank):
        sk = jnp.broadcast_to(score_ref[pl.ds(k, 1), :], shape)
        ge = jnp.where(sk >= score, 1.0, 0.0)
        gt = jnp.where(sk > score, 1.0, 0.0)
        return rank + jnp.where(blk > k, ge, gt)

    rank = lax.fori_loop(0, n_live, count, jnp.zeros(shape, F32))
    bias_t = jnp.where(rank < N_SELECT, 0.0, NEG)
    bias_t = jnp.concatenate([bias_t, jnp.zeros(shape, F32)], axis=0)
    return jnp.transpose(bias_t).astype(BF16)


def _gate_mix(gate, gexp_ref, outs):
    hi, lo = _split_bf16(gate)
    total = None
    for br, o in enumerate(outs):
        ge = _dot(hi, gexp_ref[br]) + _dot(lo, gexp_ref[br])
        total = ge * o if total is None else total + ge * o
    return total


TQ_P = 256
KEY_CHUNK = 512
SM_ROWS = 32


def _attn_prompt_body(seq, qq_ref, gate_ref, ckv_ref, kvs_ref, kvw_ref, cov_ref, et_ref, gexp_ref,
                      o_ref, score_ref, s_ref, p_ref, b_ref, psum_ref, m_ref, l_ref, acc_ref, alpha_ref):
    tq = TQ_P
    rows = GROUP_SIZE * tq
    n = pl.program_id(1)
    q0 = n * tq
    low = lax.broadcasted_iota(jnp.int32, (tq, 128), 1) < HEAD_DIM

    def q_rows(off, g):
        pieces = []
        for r in range(GROUP_SIZE):
            c0 = off + r * KV_W + (g // 2) * 128
            pair = qq_ref[:, c0:c0 + 128].astype(F32)
            if g % 2:
                pair = pltpu.roll(pair, HEAD_DIM, 1)
            pieces.append(jnp.where(low, pair, 0.0))
        return jnp.concatenate(pieces, axis=0).astype(BF16)

    def blocks(fn):
        for r0 in range(0, rows, SM_ROWS):
            fn(r0, r0 % tq)

    def block_iotas(width):
        return (lax.broadcasted_iota(jnp.int32, (SM_ROWS, width), 0),
                lax.broadcasted_iota(jnp.int32, (SM_ROWS, width), 1))

    def lanes(x, width):
        return jnp.concatenate([x] * (width // 128), axis=1)

    qc = [q_rows(0, g) for g in range(KV_GROUPS)]
    qr = [q_rows(Q_COLS, g) for g in range(KV_GROUPS)]

    def slot(g):
        return s_ref, p_ref, alpha_ref

    n_cmp = ckv_ref.shape[2]
    psum_ref[...] = jnp.zeros_like(psum_ref)
    o_c = []
    for g in range(KV_GROUPS):
        ckv = ckv_ref[0, g]
        s_g, p_g, _ = slot(g)
        s_g[:, 0:n_cmp] = _dot_t(qc[g], ckv)

        def cmp_block(r0, qoff, g=g, s_g=s_g, p_g=p_g):
            row, col = block_iotas(n_cmp)
            p = _softmax_rows(s_g[pl.ds(r0, SM_ROWS), 0:n_cmp],
                              col * CMP_STRIDE + (CMP_BLOCK - 1) <= q0 + qoff + row)
            p_g[pl.ds(r0, SM_ROWS), 0:n_cmp] = p.astype(BF16)
            psum_ref[pl.ds(g * tq + qoff, SM_ROWS), :] += p

        blocks(cmp_block)
        o_c.append(_dot(p_g[:, 0:n_cmp], ckv))
    bias = _select_bias(psum_ref[...], cov_ref[...], q0, tq,
                        (q0 + tq - 1) // SEL_BLOCK + 1, score_ref)

    m_ref[...] = jnp.full(m_ref.shape, NEG, F32)
    l_ref[...] = jnp.zeros_like(l_ref)
    acc_ref[...] = jnp.zeros_like(acc_ref)

    def chunk(c, carry, causal):
        k0 = pl.multiple_of(c * KEY_CHUNK, KEY_CHUNK)
        b_ref[...] = _dot_t(bias, et_ref[pl.ds(k0, KEY_CHUNK), :])
        for g in range(KV_GROUPS):
            kv = kvs_ref[0, g, pl.ds(k0, KEY_CHUNK), :]
            s_g, p_g, a_g = slot(g)
            s_g[:, 0:KEY_CHUNK] = _dot_t(qr[g], kv)

            def sel_block(r0, qoff, g=g, s_g=s_g, p_g=p_g, a_g=a_g):
                t = s_g[pl.ds(r0, SM_ROWS), 0:KEY_CHUNK] + b_ref[pl.ds(g * tq + qoff, SM_ROWS), :]
                if causal:
                    row, col = block_iotas(KEY_CHUNK)
                    t = jnp.where(k0 + col <= q0 + qoff + row, t, NEG)
                m_old = m_ref[g, pl.ds(r0, SM_ROWS), :]
                m_new = jnp.maximum(m_old, jnp.max(t, axis=1, keepdims=True))
                alpha = jnp.exp2(m_old - m_new)
                p = jnp.exp2(t - lanes(m_new, KEY_CHUNK))
                l_ref[g, pl.ds(r0, SM_ROWS), :] = (alpha * l_ref[g, pl.ds(r0, SM_ROWS), :]
                                                   + jnp.sum(p, axis=1, keepdims=True))
                m_ref[g, pl.ds(r0, SM_ROWS), :] = m_new
                a_g[pl.ds(r0, SM_ROWS), :] = alpha
                p_g[pl.ds(r0, SM_ROWS), 0:KEY_CHUNK] = p.astype(BF16)

            blocks(sel_block)
            acc_ref[g] = a_g[...] * acc_ref[g] + _dot(p_g[:, 0:KEY_CHUNK], kv)
        return carry

    diag = q0 // KEY_CHUNK
    lax.fori_loop(0, diag, functools.partial(chunk, causal=False), 0)
    chunk(diag, 0, True)
    o_s = [acc_ref[g] * (1.0 / l_ref[g]) for g in range(KV_GROUPS)]

    nk = WINDOW + tq
    base = pl.multiple_of(jnp.maximum(q0 - WINDOW, 0), tq)
    o_w = []
    for g in range(KV_GROUPS):
        kv = kvw_ref[0, g, pl.ds(base, nk), :]
        s_g, p_g, a_g = slot(g)
        s_g[:, 0:nk] = _dot_t(qr[g], kv)

        def win_block(r0, qoff, s_g=s_g, p_g=p_g, a_g=a_g):
            row, col = block_iotas(nk)
            kpos = base + col
            qp = q0 + qoff + row
            t = s_g[pl.ds(r0, SM_ROWS), 0:nk]
            t = jnp.where(kpos <= qp, jnp.where(kpos > qp - WINDOW, t, NEG), NEG)
            p = jnp.exp2(t - jnp.max(t, axis=1, keepdims=True))
            a_g[pl.ds(r0, SM_ROWS), :] = jnp.broadcast_to(jnp.sum(p, axis=1, keepdims=True), (SM_ROWS, 128))
            p_g[pl.ds(r0, SM_ROWS), 0:nk] = p.astype(BF16)

        blocks(win_block)
        o_w.append(_dot(p_g[:, 0:nk], kv) * (1.0 / a_g[...]))

    def to_heads(o):
        tiles = []
        for r in range(GROUP_SIZE):
            for pair in range(KV_GROUPS // 2):
                even = pltpu.roll(o[2 * pair][r * tq:(r + 1) * tq], HEAD_DIM, 1)
                tiles.append(jnp.where(low, even, o[2 * pair + 1][r * tq:(r + 1) * tq]))
        return jnp.concatenate(tiles, axis=1)

    outs = [to_heads(o) for o in (o_c, o_s, o_w)]
    o_ref[...] = _gate_mix(gate_ref[...], gexp_ref, outs).astype(BF16)


def attn_prompt(qq, gate, ckv, kvs, kvw, n_seq, seq, cov_t, e_t, gexp):
    tq = TQ_P
    nt = seq // tq
    n_cmp = ckv.shape[2]
    rows = GROUP_SIZE * tq
    max_keys = max(WINDOW + tq, KEY_CHUNK, n_cmp)
    return pl.pallas_call(
        functools.partial(_attn_prompt_body, seq),
        out_shape=jax.ShapeDtypeStruct((n_seq * seq, Q_COLS), BF16),
        grid=(n_seq, nt),
        in_specs=[
            pl.BlockSpec((tq, 2 * Q_COLS), lambda b, i: (b * nt + i, 0)),
            pl.BlockSpec((tq, GATE_PAD), lambda b, i: (b * nt + i, 0)),
            pl.BlockSpec((1, KV_GROUPS, n_cmp, 128), lambda b, i: (b, 0, 0, 0)),
            pl.BlockSpec((1, KV_GROUPS, seq, 128), lambda b, i: (b, 0, 0, 0), pipeline_mode=pl.Buffered(1)),
            pl.BlockSpec((1, KV_GROUPS, seq, 128), lambda b, i: (b, 0, 0, 0), pipeline_mode=pl.Buffered(1)),
            pl.BlockSpec(cov_t.shape, lambda b, i: (0, 0)),
            pl.BlockSpec(e_t.shape, lambda b, i: (0, 0), pipeline_mode=pl.Buffered(1)),
            pl.BlockSpec(gexp.shape, lambda b, i: (0, 0, 0)),
        ],
        out_specs=pl.BlockSpec((tq, Q_COLS), lambda b, i: (b * nt + i, 0)),
        scratch_shapes=[
            pltpu.VMEM((MAX_SEL_BLOCKS, KV_GROUPS * tq), F32),
            pltpu.VMEM((rows, max_keys), F32),
            pltpu.VMEM((rows, max_keys), BF16),
            pltpu.VMEM((KV_GROUPS * tq, KEY_CHUNK), F32),
            pltpu.VMEM((KV_GROUPS * tq, n_cmp), F32),
            pltpu.VMEM((KV_GROUPS, rows, 128), F32),
            pltpu.VMEM((KV_GROUPS, rows, 128), F32),
            pltpu.VMEM((KV_GROUPS, rows, 128), F32),
            pltpu.VMEM((rows, 128), F32),
        ],
        compiler_params=_cparams("parallel", "parallel"),
        name="attn_prompt",
    )(qq, gate, ckv, kvs, kvw, cov_t, e_t, gexp)


TQ_S = 8
NEW_PAD = 128


def _attn_sample_body(n_pages, page, n_new, pt_ref, *refs):
    ks_pages = refs[0:n_pages]
    vs_pages = refs[n_pages:2 * n_pages]
    (qq_ref, gate_ref, ck_ref, cv_ref, kvn_ref, kwn_ref, vwn_ref, kwb_ref, vwb_ref,
     cov_ref, et_ref, gexp_ref, eye_ref) = refs[2 * n_pages:2 * n_pages + 13]
    o_ref, kwo_ref, vwo_ref, score_ref = refs[-4:]
    tq = TQ_S
    rows = KV_GROUPS * GROUP_SIZE * tq
    past = n_pages * page
    gmask = _group_masks()
    qc = _stack_q(qq_ref[0, :, 0:Q_COLS], gmask)
    qr = _stack_q(qq_ref[0, :, Q_COLS:2 * Q_COLS], gmask)

    o_c, psum = _cmp_branch(qc, ck_ref[0], cv_ref[0], past, tq)
    n_live = (past + tq - 1) // SEL_BLOCK + 1
    bias = _select_bias(psum, cov_ref[...], past, tq, n_live, score_ref)

    pad = jnp.zeros((NEW_PAD - tq, KV_W), BF16)
    eye = eye_ref[...]

    def new_cols(col):
        x = jnp.concatenate([kvn_ref[0, :, col * KV_W:(col + 1) * KV_W], pad], axis=0)
        return _dot_t(eye, x).astype(BF16)

    k_all = jnp.concatenate([p[0].astype(BF16) for p in ks_pages] + [new_cols(0)], axis=1)
    v_all = jnp.concatenate([p[0].astype(BF16) for p in vs_pages] + [new_cols(1)], axis=1)
    nk = past + NEW_PAD
    s = _dot(qr, k_all) + _rep_rows(_dot_t(bias, et_ref[...]), tq)
    kpos = lax.broadcasted_iota(jnp.int32, (rows, nk), 1)
    qp = past + (lax.broadcasted_iota(jnp.int32, (rows, nk), 0) & (tq - 1))
    o_s = _dot_t(_softmax_rows(s, kpos <= qp).astype(BF16), v_all)

    wb = kwb_ref.shape[2]
    kw_all = jnp.concatenate([kwb_ref[0].astype(BF16), new_cols(2)], axis=1)
    vw_all = jnp.concatenate([vwb_ref[0].astype(BF16), new_cols(3)], axis=1)
    nw = wb + NEW_PAD
    s = _dot(qr, kw_all)
    kpos = (past - wb) + lax.broadcasted_iota(jnp.int32, (rows, nw), 1)
    qp = past + (lax.broadcasted_iota(jnp.int32, (rows, nw), 0) & (tq - 1))
    mask = (kpos >= 0) & (kpos <= qp) & (kpos > qp - WINDOW)
    o_w = _dot_t(_softmax_rows(s, mask).astype(BF16), vw_all)

    outs = [_unstack_o(o, tq, gmask) for o in (o_c, o_s, o_w)]
    o_ref[0] = _gate_mix(gate_ref[0], gexp_ref, outs).astype(BF16)

    last = lax.broadcasted_iota(jnp.int32, (KV_W, 128), 1) >= 128 - n_new
    fill = jnp.zeros((128 - tq, KV_W), F32)
    for new_ref, buf_ref, out_ref in ((kwn_ref, kwb_ref, kwo_ref), (vwn_ref, vwb_ref, vwo_ref)):
        tail = jnp.concatenate([fill, pltpu.roll(new_ref[0], tq - n_new, 0)], axis=0)
        hi, lo = _split_bf16(tail)
        tail_t = _dot_t(eye, hi) + _dot_t(eye, lo)
        shifted = pltpu.roll(buf_ref[0], wb - n_new, 1)
        out_ref[0, :, 0:wb - 128] = shifted[:, 0:wb - 128]
        out_ref[0, :, wb - 128:wb] = jnp.where(last, tail_t, shifted[:, wb - 128:wb])


def attn_sample(pool_ks, pool_vs, page0, page_table, qq, gate, ck, cv, kvn, kwn, vwn, kwb, vwb, win0, n_new,
                cov_t, e_t, gexp, eye, prev=()):
    nb, n_pages = page_table.shape
    page = pool_ks.shape[2]
    wb = kwb.shape[2]
    n_cmp = ck.shape[1]
    assert page == 128 and wb % 128 == 0

    def page_spec(p):
        return pl.BlockSpec((1, KV_W, page), lambda b, pt: (page0 + pt[b * n_pages + p], 0, 0))

    def bspec(shape):
        return pl.BlockSpec((1,) + shape, lambda b, pt: (b, 0, 0))

    def cspec(shape):
        return pl.BlockSpec(shape, lambda b, pt: (0,) * len(shape))

    win_spec = pl.BlockSpec((1, KV_W, wb), lambda b, pt: (win0 + b, 0, 0))
    grid_spec = pltpu.PrefetchScalarGridSpec(
        num_scalar_prefetch=1,
        grid=(nb,),
        in_specs=[page_spec(p) for p in range(n_pages)] * 2 + [
            bspec((TQ_S, 2 * Q_COLS)), bspec((TQ_S, GATE_PAD)), bspec((n_cmp, KV_W)), bspec((n_cmp, KV_W)),
            bspec((TQ_S, 4 * KV_W)), bspec((TQ_S, KV_W)), bspec((TQ_S, KV_W)),
            win_spec, win_spec,
            cspec(cov_t.shape), cspec(e_t.shape), cspec(gexp.shape), cspec(eye.shape),
        ] + [pl.BlockSpec(memory_space=pl.ANY)] * len(prev),
        out_specs=(bspec((TQ_S, Q_COLS)), win_spec, win_spec),
        scratch_shapes=[pltpu.VMEM((MAX_SEL_BLOCKS, KV_GROUPS * TQ_S), F32)],
    )
    n_in = 1 + 2 * n_pages + 13
    return pl.pallas_call(
        functools.partial(_attn_sample_body, n_pages, page, n_new),
        out_shape=(jax.ShapeDtypeStruct((nb, TQ_S, Q_COLS), BF16),
                   jax.ShapeDtypeStruct(kwb.shape, F32),
                   jax.ShapeDtypeStruct(vwb.shape, F32)),
        grid_spec=grid_spec,
        input_output_aliases={n_in + k: 1 + k for k in range(len(prev))},
        compiler_params=_cparams("parallel"),
        name="attn_sample",
    )(page_table.reshape(-1), *([pool_ks] * n_pages), *([pool_vs] * n_pages),
      qq, gate, ck, cv, kvn, kwn, vwn, kwb, vwb, cov_t, e_t, gexp, eye, *prev)


def _rope_tables(pos):
    half = HEAD_DIM // 2
    inv = ROPE_THETA ** (-jnp.arange(half, dtype=F32) / half)
    ang = pos.astype(F32)[:, None] * inv[None, :]
    c, s = jnp.cos(ang), jnp.sin(ang)
    return jnp.tile(jnp.concatenate([c, c], axis=1), (1, 2)), jnp.tile(jnp.concatenate([-s, s], axis=1), (1, 2))


def _head_perm():
    c = np.arange(Q_COLS)
    r, g, d = c // KV_W, (c % KV_W) // HEAD_DIM, c % HEAD_DIM
    return (g * GROUP_SIZE + r) * HEAD_DIM + d


def _cover_t(n_cmp, n_cmp_pad):
    cs = np.arange(n_cmp_pad) * CMP_STRIDE
    ss = np.arange(MAX_SEL_BLOCKS) * SEL_BLOCK
    m = (cs[None, :] < ss[:, None] + SEL_BLOCK) & (cs[None, :] + CMP_BLOCK > ss[:, None])
    m = m & (np.arange(n_cmp_pad)[None, :] < n_cmp)
    return jnp.asarray(m, dtype=BF16)


def _block_onehot_t(n_keys):
    k = np.arange(n_keys)
    return jnp.asarray((k[:, None] // SEL_BLOCK) == np.arange(128)[None, :], dtype=BF16)


def _gate_expand():
    ge = np.zeros((N_BRANCH, GATE_PAD, Q_COLS), np.float32)
    c = np.arange(Q_COLS)
    r, g = c // KV_W, (c % KV_W) // HEAD_DIM
    for br in range(N_BRANCH):
        ge[br, (g * GROUP_SIZE + r) * N_BRANCH + br, c] = 1.0
    return jnp.asarray(ge, dtype=BF16)


def _compress_weights(pe, w1, w2):
    half = CMP_BLOCK // 2 * HEAD_DIM
    w1cat = jnp.concatenate([w1[:half], w1[half:]], axis=1).astype(BF16)
    pe2 = jnp.pad(pe.reshape(2, half), ((0, 6), (0, 0)))
    w2bd = jnp.kron(jnp.eye(KV_GROUPS, dtype=F32), w2).astype(BF16)
    return w1cat, pe2, w2bd


def _row(v):
    return v.reshape(1, -1)


TM_PROMPT = 512
TM_SAMPLE = 512
TM_FFN_PROMPT = 1024


def kernel(x_prompt, x_sample, state_conv, cache_k_cmp, cache_v_cmp, cache_k_sel, cache_v_sel, state_k_win, state_v_win, page_table, p_prompt, p_sample, norm_mix_pre, norm_mix_post, norm_ffn_pre, norm_ffn_post, conv_w_in, conv_b_in, conv_w_dw, conv_b_dw, conv_ln_g, conv_ln_b, conv_w_out, conv_b_out, nsa_w_in, nsa_b_gate, nsa_pe_k, nsa_pe_v, nsa_wk_c1, nsa_wk_c2, nsa_wv_c1, nsa_wv_c2, nsa_w_o, ffn_w_in, ffn_w_out, ple_w_gate, ple_w_proj):
    n_p, seq, _ = x_prompt.shape
    n_s, n_new, _ = x_sample.shape
    n_pages = page_table.shape[1]
    page = cache_k_cmp.shape[2]
    past = n_pages * page
    tp, ts = n_p * seq, n_s * n_new
    assert seq % KEY_CHUNK == 0 and seq // SEL_BLOCK <= MAX_SEL_BLOCKS and seq >= WINDOW + TQ_P
    assert (past + n_new + SEL_BLOCK - 1) // SEL_BLOCK <= MAX_SEL_BLOCKS and n_new <= TQ_S
    assert past % SEL_BLOCK == 0 and page % CMP_STRIDE == 0 and state_k_win.shape[2] == WINDOW

    hp = x_prompt.reshape(tp, D_MODEL)
    hs = x_sample.reshape(ts, D_MODEL)
    pp = p_prompt.reshape(DEPTH, tp, -1)
    ps = p_sample.reshape(DEPTH, ts, -1)

    perm = _head_perm()
    gexp = _gate_expand()
    cos_p, sin_p = _rope_tables(jnp.arange(seq))
    cos_s, sin_s = _rope_tables(past + jnp.arange(ts) % n_new)
    n_cmp_p = seq // CMP_STRIDE - 1
    cov_p = _cover_t(n_cmp_p, seq // CMP_STRIDE)
    n_cmp_s = past // CMP_STRIDE
    cov_s = _cover_t(n_cmp_s, n_cmp_s)
    et_p = _block_onehot_t(seq)
    et_s = _block_onehot_t(past + NEW_PAD)
    eye = jnp.eye(KV_W, dtype=BF16)
    tiles_per_seq = seq // TM_PROMPT

    def pad_rows(x, n):
        return jnp.pad(x, ((0, 0), (0, n - x.shape[1]), (0, 0)))

    def token_minor(c):
        return jnp.transpose(c, (0, 1, 3, 4, 2)).reshape(c.shape[0] * c.shape[1], KV_W, c.shape[2])

    def token_major(t, lead):
        t = t.reshape(lead + (KV_GROUPS, HEAD_DIM, t.shape[-1]))
        return jnp.moveaxis(t, -1, len(lead))

    pools = [token_minor(c) for c in (cache_k_cmp, cache_v_cmp, cache_k_sel, cache_v_sel)]
    n_pool = cache_k_cmp.shape[1]
    kwin_t, vwin_t = token_minor(state_k_win), token_minor(state_v_win)
    conv_state_t = jnp.transpose(state_conv, (0, 2, 1, 3)).reshape(-1, n_s, D_MODEL)

    conv_p, conv_s, nsa_s = [], [], []
    n_nsa = DEPTH // N_MIXERS
    state_p, win_s = [], []
    for i in range(DEPTH):
        j = i // N_MIXERS
        g_pre, g_post = _row(norm_mix_pre[i]), _row(norm_mix_post[i])
        if i % N_MIXERS == 0:
            w_in = conv_w_in[j].astype(BF16)
            b_in = _row(conv_b_in[j])
            w_out = conv_w_out[j].astype(BF16)
            dw = (conv_w_dw[j], _row(conv_b_dw[j]), _row(conv_ln_g[j]), _row(conv_ln_b[j]))
            glu_p = conv_in(hp, g_pre, w_in, b_in, TM_PROMPT)
            glu_s = conv_in(hs, g_pre, w_in, b_in, TM_SAMPLE)
            z_p = dwconv_prompt(glu_p, n_p, seq, *dw)
            z_s, st_s = dwconv_sample(conv_state_t, j, glu_s.reshape(n_s, n_new, D_MODEL), *dw)
            conv_p.append(glu_p.reshape(n_p, seq, D_MODEL)[:, seq - (CONV_WIDTH - 1):])
            conv_s.append(st_s)
            b_out = _row(conv_b_out[j])
            hp = out_proj_res(z_p, w_out, b_out, g_post, hp, TM_PROMPT)
            hs = out_proj_res(z_s.reshape(ts, D_MODEL), w_out, b_out, g_post, hs, TM_SAMPLE)
        else:
            w = nsa_w_in[j]
            w_in = jnp.concatenate([w[:, :Q_COLS][:, perm], w[:, Q_COLS:],
                                    jnp.zeros((D_MODEL, GATE_PAD - GATE_COLS), F32)], axis=1).astype(BF16)
            bg = jnp.pad(nsa_b_gate[j], (0, GATE_PAD - GATE_COLS)).reshape(1, GATE_PAD)
            w_o = nsa_w_o[j][perm].astype(BF16)
            wk = _compress_weights(nsa_pe_k[j], nsa_wk_c1[j], nsa_wk_c2[j])
            wv = _compress_weights(nsa_pe_v[j], nsa_wv_c1[j], nsa_wv_c2[j])
            zero_b = jnp.zeros((1, D_MODEL), F32)

            qq, kc, vc, *state_p, kvs, kvw, gate = nsa_in(
                hp, g_pre, w_in, bg, cos_p, sin_p, lambda t: t % tiles_per_seq, TM_PROMPT, seq=seq,
                layer=j, n_layers=n_nsa, prev=state_p)
            zero_w2 = jnp.zeros((CMP_HIDDEN, HEAD_DIM), F32)
            w2_packed = jnp.block([[nsa_wk_c2[j], zero_w2], [zero_w2, nsa_wv_c2[j]]]).astype(BF16)
            ckv = compress_prompt(kc, vc, n_p, seq, wk, wv, w2_packed)
            o_p = attn_prompt(qq, gate, ckv, kvs, kvw, n_p, seq, cov_p, et_p, gexp)
            hp = out_proj_res(o_p, w_o, zero_b, g_post, hp, TM_PROMPT)

            qq, kc, vc, ks, vs, kw, vw, kvb, gate = nsa_in(
                hs, g_pre, w_in, bg, cos_s, sin_s, lambda t: t, TM_SAMPLE)
            r3 = lambda x: x.reshape(n_s, n_new, x.shape[-1])
            ck, cv = compress_sample(pools[0], pools[1], j * n_pool, page_table,
                                     pad_rows(r3(kc), CMP_STRIDE), pad_rows(r3(vc), CMP_STRIDE), wk, wv)
            o_s, *win_s = attn_sample(
                pools[2], pools[3], j * n_pool, page_table,
                pad_rows(r3(qq), TQ_S), pad_rows(r3(gate), TQ_S), ck, cv,
                pad_rows(r3(kvb), TQ_S), pad_rows(r3(kw), TQ_S), pad_rows(r3(vw), TQ_S),
                kwin_t, vwin_t, j * n_s, n_new, cov_s, et_s, gexp, eye, prev=win_s)
            hs = out_proj_res(o_s[:, :n_new].reshape(ts, Q_COLS), w_o, zero_b, g_post, hs, TM_SAMPLE)
            kvshape = (n_s, n_new, KV_GROUPS, HEAD_DIM)
            nsa_s.append((kc.reshape(kvshape), vc.reshape(kvshape), ks.reshape(kvshape), vs.reshape(kvshape)))

        fw = (_row(norm_ffn_pre[i]), _row(norm_ffn_post[i]), ffn_w_in[i].astype(BF16), ffn_w_out[i].astype(BF16),
              ple_w_gate[i].astype(BF16), ple_w_proj[i].astype(BF16))
        hp = ffn_ple(hp, pp[i], *fw, TM_FFN_PROMPT)
        hs = ffn_ple(hs, ps[i], *fw, TM_SAMPLE)

    stack = lambda lst, k: jnp.stack([st[k] for st in lst])
    conv_state_s = jnp.transpose(jnp.stack(conv_s), (0, 2, 1, 3))
    keep = min(WINDOW, seq)
    kc_t, vc_t, ks_t, vs_t, kw_t, vw_t = [token_major(t, (n_nsa, n_p)) for t in state_p]
    kwin_s, vwin_s = [token_major(t.reshape(n_nsa, n_s, KV_W, t.shape[-1]), (n_nsa, n_s)) for t in win_s]
    return (hp.reshape(x_prompt.shape), hs.reshape(x_sample.shape), jnp.stack(conv_p),
            kc_t, vc_t, ks_t, vs_t, kw_t[:, :, seq - keep:], vw_t[:, :, seq - keep:],
            conv_state_s,
            stack(nsa_s, 0), stack(nsa_s, 1), stack(nsa_s, 2), stack(nsa_s, 3),
            kwin_s, vwin_s)
```
